```python
import math
import jax, jax.numpy as jnp
from jax import lax
import numpy as np

D_MODEL = 1024
BATCH = 8
SEQ = 4096
DEPTH = 2

MEM_LEN = 256
BLOCK_Q = 128
H_A = 8
DH_A = 64
DC_A = 128
H_IDX = 8
DH_IDX = 64
TOPK_MAX = 256
H_SB = 8
DH_SB = 64
C_CONV = 512
CONV_W = 3
W_A = H_A * DH_A
W_SB = H_SB * DH_SB
W_BRANCH = 512
N_BRANCH = 3
H_X = 4
DH_X = 128
D_FF = 2816
N_BUCKETS = 32
MAX_DISTANCE = 128
DN_ALPHA = (2 * DEPTH) ** 0.25
DN_BETA = (8 * DEPTH) ** -0.25
LN_EPS = 1e-5
N_LN = 4
SPLITS = (W_A, DC_A, H_IDX * DH_IDX, DH_IDX, H_IDX, W_SB, W_SB, W_SB,
          C_CONV, C_CONV, C_CONV, D_MODEL, D_MODEL, D_MODEL)
P_TOTAL = sum(SPLITS)

kernel_name = "hybrid_dsa_stickbreak_shortconv_gated"


def layer_norm(x, g, b):
    xf = x.astype(jnp.float32)
    mu = jnp.mean(xf, axis=-1, keepdims=True)
    var = jnp.mean(jnp.square(xf - mu), axis=-1, keepdims=True)
    y = (xf - mu) * lax.rsqrt(var + LN_EPS) * g.astype(jnp.float32) + b.astype(jnp.float32)
    return y.astype(x.dtype)


def rms_norm(x, g):
    xf = x.astype(jnp.float32)
    y = xf * lax.rsqrt(jnp.mean(jnp.square(xf), axis=-1, keepdims=True) + LN_EPS) * g.astype(jnp.float32)
    return y.astype(x.dtype)


def swiglu_ffn(x, w_in, w_out):
    a, b = jnp.split(x @ w_in, 2, axis=-1)
    return (jax.nn.silu(a) * b) @ w_out


def t5_bucket(n):
    max_exact = N_BUCKETS // 2
    n = jnp.maximum(n, 0)
    nf = jnp.maximum(n, 1).astype(jnp.float32)
    large = max_exact + (jnp.log(nf / max_exact) / math.log(MAX_DISTANCE / max_exact)
                         * (N_BUCKETS - max_exact)).astype(jnp.int32)
    large = jnp.minimum(large, N_BUCKETS - 1)
    return jnp.where(n < max_exact, n, large)


def dsa_branch(q, c_kv, iq, ik, iw, w_uk, w_uv, rel_bias):
    bsz, seq = q.shape[0], q.shape[1]
    k_top = min(TOPK_MAX, seq // 4)
    nblk = seq // BLOCK_Q
    q_lat = jnp.einsum('bshd,hdc->bshc', q, w_uk) * (DH_A ** -0.5)
    idx_scale = (H_IDX ** -0.5) * (DH_IDX ** -0.5)

    def to_blocks(a):
        return jnp.swapaxes(a.reshape((bsz, nblk, BLOCK_Q) + a.shape[2:]), 0, 1)

    starts = jnp.arange(nblk, dtype=jnp.int32) * BLOCK_Q
    key_pos = jnp.arange(seq, dtype=jnp.int32)

    def block(args):
        ql, iqb, iwb, start = args
        t = start + jnp.arange(BLOCK_Q, dtype=jnp.int32)
        causal = key_pos[None, :] <= t[:, None]
        dots = jnp.einsum('bqhd,bsd->bhqs', iqb, ik).astype(jnp.float32)
        score = jnp.einsum('bqh,bhqs->bqs', iwb.astype(jnp.float32) * idx_scale, jax.nn.relu(dots))
        score = jnp.where(causal[None], score, -jnp.inf)
        _, idx = lax.top_k(score, k_top)
        valid = idx <= t[None, :, None]
        kv_sel = jax.vmap(lambda c, i: c[i])(c_kv, idx)
        logits = jnp.einsum('bqhc,bqkc->bhqk', ql, kv_sel).astype(jnp.float32)
        bias = rel_bias[t5_bucket(t[None, :, None] - idx)]
        logits = logits + jnp.transpose(bias, (0, 3, 1, 2)).astype(jnp.float32)
        logits = jnp.where(valid[:, None], logits, -1e30)
        p = jax.nn.softmax(logits, axis=-1).astype(kv_sel.dtype)
        return jnp.einsum('bhqk,bqkc->bqhc', p, kv_sel)

    o_lat = lax.map(block, (to_blocks(q_lat), to_blocks(iq), to_blocks(iw), starts))
    o_lat = jnp.swapaxes(o_lat, 0, 1).reshape(bsz, seq, H_A, DC_A)
    o = jnp.einsum('bshc,hcd->bshd', o_lat, w_uv)
    return o.reshape(bsz, seq, W_A)


def stick_breaking_branch(q, k, v):
    bsz, seq = q.shape[0], q.shape[1]
    scale = DH_SB ** -0.5
    outs = []
    for i in range(seq // BLOCK_Q):
        q0 = i * BLOCK_Q
        end = q0 + BLOCK_Q
        z = jnp.einsum('bqhd,bshd->bhqs', q[:, q0:end], k[:, :end]).astype(jnp.float32) * scale
        t = q0 + jnp.arange(BLOCK_Q)
        s = jnp.arange(end)
        strict = (s[None, :] < t[:, None])[None, None]
        log_not = jnp.where(strict, jax.nn.log_sigmoid(-z), 0.0)
        later = lax.cumsum(log_not, axis=3, reverse=True) - log_not
        a = jnp.where(strict, jnp.exp(jax.nn.log_sigmoid(z) + later), 0.0)
        outs.append(jnp.einsum('bhqs,bshd->bqhd', a.astype(v.dtype), v[:, :end]))
    return jnp.concatenate(outs, axis=1).reshape(bsz, seq, W_SB)


def short_conv_branch(b_gate_in, c_gate_in, h, conv_w):
    z = c_gate_in * h
    y = lax.conv_general_dilated(z, conv_w[:, None, :].astype(z.dtype), window_strides=(1,),
                                 padding=((CONV_W - 1, 0),),
                                 dimension_numbers=('NWC', 'WIO', 'NWC'),
                                 feature_group_count=C_CONV)
    return b_gate_in * y


def mixer_sublayer(x, w_in, b_gate, kv_g, w_uk, w_uv, conv_w, w_branch, w_out, rel_bias):
    bsz, seq = x.shape[0], x.shape[1]
    offsets = [int(o) for o in np.cumsum(SPLITS)[:-1]]
    (qa, ckv, iq, ik, iw, qs, ks, vs, cb, cc, ch, ga, gb, gc) = jnp.split(x @ w_in, offsets, axis=-1)
    ckv = rms_norm(ckv, kv_g)
    y_a = dsa_branch(qa.reshape(bsz, seq, H_A, DH_A), ckv,
                     iq.reshape(bsz, seq, H_IDX, DH_IDX), ik, iw, w_uk, w_uv, rel_bias)
    hs = (bsz, seq, H_SB, DH_SB)
    y_b = stick_breaking_branch(qs.reshape(hs), ks.reshape(hs), vs.reshape(hs))
    y_c = short_conv_branch(cb, cc, ch, conv_w)
    merged = (jax.nn.sigmoid(ga + b_gate[0]) * (y_a @ w_branch[0])
              + jax.nn.sigmoid(gb + b_gate[1]) * (y_b @ w_branch[1])
              + jax.nn.sigmoid(gc + b_gate[2]) * (y_c @ w_branch[2]))
    return merged @ w_out


def memory_cross_attention(x, mem, wq, wkv, wo):
    bsz, seq = x.shape[0], x.shape[1]
    q = (x @ wq).reshape(bsz, seq, H_X, DH_X)
    k, v = jnp.split(mem @ wkv, 2, axis=-1)
    k = k.reshape(bsz, mem.shape[1], H_X, DH_X)
    v = v.reshape(bsz, mem.shape[1], H_X, DH_X)
    logits = jnp.einsum('bqhd,bmhd->bhqm', q, k).astype(jnp.float32) * (DH_X ** -0.5)
    p = jax.nn.softmax(logits, axis=-1).astype(v.dtype)
    o = jnp.einsum('bhqm,bmhd->bqhd', p, v).reshape(bsz, seq, H_X * DH_X)
    return o @ wo


def setup_inputs(seed: int = 0) -> dict:
    key = jax.random.key(seed)
    ks = jax.random.split(key, 20)
    n = jax.random.normal
    f32 = jnp.float32
    return {
        "x": n(ks[0], (BATCH, SEQ, D_MODEL), f32),
        "mem": n(ks[1], (BATCH, MEM_LEN, D_MODEL), f32),
        "ln_g": 1.0 + 0.01 * n(ks[2], (DEPTH, N_LN, D_MODEL), f32),
        "ln_b": 0.01 * n(ks[3], (DEPTH, N_LN, D_MODEL), f32),
        "ffn_w_in": n(ks[4], (DEPTH, 2, D_MODEL, 2 * D_FF), f32) * D_MODEL ** -0.5,
        "ffn_w_out": n(ks[5], (DEPTH, 2, D_FF, D_MODEL), f32) * (D_FF ** -0.5 * DN_BETA),
        "w_mix_in": n(ks[6], (DEPTH, D_MODEL, P_TOTAL), f32) * D_MODEL ** -0.5,
        "b_gate": 0.01 * n(ks[7], (DEPTH, N_BRANCH, D_MODEL), f32),
        "kv_norm_g": 1.0 + 0.01 * n(ks[8], (DEPTH, DC_A), f32),
        "w_uk": n(ks[9], (DEPTH, H_A, DH_A, DC_A), f32) * DH_A ** -0.5,
        "w_uv": n(ks[10], (DEPTH, H_A, DC_A, DH_A), f32) * DC_A ** -0.5,
        "conv_w": n(ks[11], (DEPTH, CONV_W, C_CONV), f32) * CONV_W ** -0.5,
        "w_branch": n(ks[12], (DEPTH, N_BRANCH, W_BRANCH, D_MODEL), f32) * W_BRANCH ** -0.5,
        "w_mix_out": n(ks[13], (DEPTH, D_MODEL, D_MODEL), f32) * (D_MODEL ** -0.5 * DN_BETA),
        "xa_wq": n(ks[14], (DEPTH, D_MODEL, H_X * DH_X), f32) * D_MODEL ** -0.5,
        "xa_wkv": n(ks[15], (DEPTH, D_MODEL, 2 * H_X * DH_X), f32) * D_MODEL ** -0.5,
        "xa_wo": n(ks[16], (DEPTH, H_X * DH_X, D_MODEL), f32) * ((H_X * DH_X) ** -0.5 * DN_BETA),
        "rel_bias": 0.1 * n(ks[17], (N_BUCKETS, H_A), f32),
    }


def reference(x, mem, ln_g, ln_b, ffn_w_in, ffn_w_out, w_mix_in, b_gate, kv_norm_g, w_uk, w_uv,
              conv_w, w_branch, w_mix_out, xa_wq, xa_wkv, xa_wo, rel_bias):
    for l in range(DEPTH):
        x = layer_norm(DN_ALPHA * x + 0.5 * swiglu_ffn(x, ffn_w_in[l, 0], ffn_w_out[l, 0]),
                       ln_g[l, 0], ln_b[l, 0])
        x = layer_norm(DN_ALPHA * x + mixer_sublayer(x, w_mix_in[l], b_gate[l], kv_norm_g[l], w_uk[l],
                                                     w_uv[l], conv_w[l], w_branch[l], w_mix_out[l],
                                                     rel_bias),
                       ln_g[l, 1], ln_b[l, 1])
        x = layer_norm(DN_ALPHA * x + memory_cross_attention(x, mem, xa_wq[l], xa_wkv[l], xa_wo[l]),
                       ln_g[l, 2], ln_b[l, 2])
        x = layer_norm(DN_ALPHA * x + 0.5 * swiglu_ffn(x, ffn_w_in[l, 1], ffn_w_out[l, 1]),
                       ln_g[l, 3], ln_b[l, 3])
    return x
```

```python
import functools
import math

import jax
import jax.numpy as jnp
import numpy as np
from jax import lax
from jax.experimental import pallas as pl
from jax.experimental.pallas import tpu as pltpu

BLOCK_Q = 128
H_A, DH_A, DC_A = 8, 64, 128
H_IDX, DH_IDX = 8, 64
TOPK_MAX = 256
H_SB, DH_SB = 8, 64
C_CONV, CONV_W = 512, 3
W_BRANCH = 512
H_X, DH_X = 4, 128
N_BUCKETS, MAX_DISTANCE = 32, 128
LN_EPS = 1e-5

V7X_LANES = 128
V7X_SUBLANES = 8
V7X_VMEM_LIMIT_BYTES = 56 * 1024 * 1024

_F32 = jnp.float32
_BF16 = jnp.bfloat16
_I32 = jnp.int32

_INT_MIN = -(2 ** 31)
_NEG_INF_KEY = np.int32(np.uint32(0x807FFFFF).view(np.int32))
_MASK_VALUE = -1e30
_SB_DEAD_LOG = -110.0


def _dot(a, b):
    return jnp.dot(a, b, preferred_element_type=_F32)


def _dot_nt(a, b):
    return lax.dot_general(a, b, (((1,), (1,)), ((), ())), preferred_element_type=_F32)


def _dot_tn(a, b):
    return lax.dot_general(a, b, (((0,), (0,)), ((), ())), preferred_element_type=_F32)


def _layer_norm(y, g, b):
    mu = jnp.mean(y, axis=-1, keepdims=True)
    yc = y - mu
    var = jnp.mean(yc * yc, axis=-1, keepdims=True)
    return yc * lax.rsqrt(var + LN_EPS) * g + b


def _params(n_grid, **kw):
    return pltpu.CompilerParams(
        dimension_semantics=("arbitrary",) * n_grid,
        vmem_limit_bytes=V7X_VMEM_LIMIT_BYTES, **kw)


def _const_spec(shape):
    n = len(shape)
    return pl.BlockSpec(shape, lambda *_: (0,) * n, pipeline_mode=pl.Buffered(1))


def _ffn_ln_kernel(x_ref, wa_ref, wb_ref, wo_ref, g_ref, b_ref, o_ref, *, alpha, tf):
    x = x_ref[...]
    xb = x.astype(_BF16)
    d_ff = wa_ref.shape[1]
    acc = jnp.zeros(x.shape, _F32)
    for c in range(d_ff // tf):
        sl = slice(c * tf, (c + 1) * tf)
        a = _dot(xb, wa_ref[:, sl])
        b = _dot(xb, wb_ref[:, sl])
        h = (a * jax.nn.sigmoid(a) * b).astype(_BF16)
        acc = acc + _dot(h, wo_ref[sl, :])
    y = alpha * x + 0.5 * acc
    o_ref[...] = _layer_norm(y, g_ref[...], b_ref[...])


def _ffn_ln(x, wa, wb, wo, g, b, *, alpha, tm, tf):
    t, d = x.shape
    f = wa.shape[1]
    return pl.pallas_call(
        functools.partial(_ffn_ln_kernel, alpha=alpha, tf=tf),
        out_shape=jax.ShapeDtypeStruct((t, d), _F32),
        grid=(t // tm,),
        in_specs=[pl.BlockSpec((tm, d), lambda i: (i, 0)),
                  _const_spec((d, f)), _const_spec((d, f)), _const_spec((f, d)),
                  _const_spec((1, d)), _const_spec((1, d))],
        out_specs=pl.BlockSpec((tm, d), lambda i: (i, 0)),
        compiler_params=_params(1),
        name="ffn_ln",
    )(x, wa, wb, wo, g, b)


def _mix_proj_kernel(x_ref, w_ref, kvg_ref, qa_ref, iq_ref, qs_ref, ks_ref, vs_ref, ckv_ref, ikw_ref):
    xb = x_ref[...].astype(_BF16)
    wide = W_BRANCH
    for j, ref in enumerate((qa_ref, iq_ref, qs_ref, ks_ref, vs_ref)):
        ref[...] = _dot(xb, w_ref[:, j * wide:(j + 1) * wide]).astype(_BF16)
    off = 5 * wide
    ckv = _dot(xb, w_ref[:, off:off + DC_A])
    ckv = ckv * lax.rsqrt(jnp.mean(ckv * ckv, axis=-1, keepdims=True) + LN_EPS) * kvg_ref[...]
    ckv_ref[...] = ckv.astype(_BF16)
    ikw_ref[...] = _dot(xb, w_ref[:, off + DC_A:off + DC_A + V7X_LANES])


def _mix_proj(x, w, kvg, *, tm):
    t, d = x.shape
    n = w.shape[1]
    wide_spec = pl.BlockSpec((tm, W_BRANCH), lambda i: (i, 0))
    lane_spec = pl.BlockSpec((tm, V7X_LANES), lambda i: (i, 0))
    return pl.pallas_call(
        _mix_proj_kernel,
        out_shape=[jax.ShapeDtypeStruct((t, W_BRANCH), _BF16)] * 5
        + [jax.ShapeDtypeStruct((t, DC_A), _BF16), jax.ShapeDtypeStruct((t, V7X_LANES), _F32)],
        grid=(t // tm,),
        in_specs=[pl.BlockSpec((tm, d), lambda i: (i, 0)), _const_spec((d, n)), _const_spec((1, DC_A))],
        out_specs=[wide_spec] * 5 + [lane_spec, lane_spec],
        compiler_params=_params(1),
        name="mix_proj",
    )(x, w, kvg)


def _dsa_kernel(qa_ref, iq_ref, iwt_ref, ik_ref, ckv_ref, wukt_ref, wuv_ref, bias_ref, tri_ref,
                o_ref, keys_ref, p_ref, acc_ref, m_ref, l_ref, *, k_top, kc):
    bq = BLOCK_Q
    i = pl.program_id(1)
    q0 = i * bq
    n_chunks = (q0 + bq + kc - 1) // kc
    sub = kc // bq

    qa = qa_ref[0]
    iq = iq_ref[0]
    iq_rows = jnp.concatenate([iq[:, h * DH_IDX:(h + 1) * DH_IDX] for h in range(H_IDX)], axis=0)
    w_idx = iwt_ref[0] * ((H_IDX ** -0.5) * (DH_IDX ** -0.5))
    t_idx = q0 + lax.broadcasted_iota(_I32, (1, bq), 1)

    def score_chunk(c, carry):
        s0 = pl.multiple_of(c * kc, kc)
        d = _dot_nt(ik_ref[0, pl.ds(s0, kc), :], iq_rows)
        sc = jnp.zeros((kc, bq), _F32)
        for h in range(H_IDX):
            sc = sc + jnp.maximum(d[:, h * bq:(h + 1) * bq], 0.0) * w_idx[h:h + 1, :]
        s_idx = s0 + lax.broadcasted_iota(_I32, (kc, bq), 0)
        sc = jnp.where(s_idx <= t_idx, sc, -jnp.inf)
        bits = pltpu.bitcast(sc, _I32)
        keys_ref[pl.ds(s0, kc), :] = bits ^ ((bits >> 31) & 0x7FFFFFFF)
        return carry

    lax.fori_loop(0, n_chunks, score_chunk, 0)

    def count_ge(cand):
        def body(c, cnt):
            s0 = pl.multiple_of(c * kc, kc)
            ge = jnp.where(keys_ref[pl.ds(s0, kc), :] >= cand, 1, 0)
            return cnt + jnp.sum(ge.reshape(kc // V7X_SUBLANES, V7X_SUBLANES, bq), axis=0)
        cnt = lax.fori_loop(0, n_chunks, body, jnp.zeros((V7X_SUBLANES, bq), _I32))
        return jnp.sum(cnt, axis=0, keepdims=True)

    zero_row = jnp.zeros((1, bq), _I32)
    thr0 = jnp.where(count_ge(zero_row) >= k_top, zero_row, _INT_MIN)

    def bit_step(b, thr):
        cand = thr + lax.shift_left(jnp.int32(1), 30 - b)
        return jnp.where(count_ge(cand) >= k_top, cand, thr)

    thr = lax.fori_loop(0, 31, bit_step, thr0)
    n_gt = count_ge(thr + 1)
    n_ties = jnp.where(thr > _NEG_INF_KEY, k_top - n_gt, 0).astype(_F32)
    thr_gt = jnp.maximum(thr, _NEG_INF_KEY)

    q_lat_t = jnp.concatenate(
        [_dot_nt(wukt_ref[h], qa[:, h * DH_A:(h + 1) * DH_A]) for h in range(H_A)], axis=1)
    q_lat_t = (q_lat_t * (DH_A ** -0.5)).astype(_BF16)

    m_ref[...] = jnp.full(m_ref.shape, -jnp.inf, _F32)
    l_ref[...] = jnp.zeros(l_ref.shape, _F32)
    acc_ref[...] = jnp.zeros(acc_ref.shape, _F32)

    def attend_chunk(c, ties_before):
        s0 = pl.multiple_of(c * kc, kc)
        keys = keys_ref[pl.ds(s0, kc), :]
        eq = keys == thr
        eq_b = jnp.where(eq, 1.0, 0.0).astype(_BF16)
        rank = ties_before + _dot(tri_ref[...], eq_b)
        sel = (keys > thr_gt) | (eq & (rank < n_ties))
        mask_add = jnp.where(sel, 0.0, _MASK_VALUE)
        ties_before = ties_before + jnp.sum(eq_b.astype(_F32), axis=0, keepdims=True)

        ckv_c = ckv_ref[0, pl.ds(s0, kc), :]
        logits = _dot(ckv_c, q_lat_t)
        alphas = []
        for h in range(H_A):
            hs = slice(h * bq, (h + 1) * bq)
            parts = []
            for j in range(sub):
                d_blk = jnp.clip(i - (c * sub + j), 0, 2)
                rs = slice(j * bq, (j + 1) * bq)
                parts.append(logits[rs, hs] + bias_ref[d_blk, h] + mask_add[rs, :])
            lt = jnp.concatenate(parts, axis=0) if sub > 1 else parts[0]
            m_old = m_ref[:, hs]
            m_new = jnp.maximum(m_old, jnp.max(lt, axis=0, keepdims=True))
            alpha = jnp.exp(m_old - m_new)
            p = jnp.exp(lt - m_new)
            l_ref[:, hs] = alpha * l_ref[:, hs] + jnp.sum(p, axis=0, keepdims=True)
            m_ref[:, hs] = m_new
            p_ref[:, hs] = p.astype(_BF16)
            alphas.append(alpha)
        alpha_row = jnp.concatenate(alphas, axis=1)
        acc_ref[...] = acc_ref[...] * alpha_row + _dot_tn(ckv_c, p_ref[...])
        return ties_before

    lax.fori_loop(0, n_chunks, attend_chunk, jnp.zeros((1, bq), _F32))

    o_lat_t = (acc_ref[...] * (1.0 / l_ref[...])).astype(_BF16)
    outs = [_dot_tn(o_lat_t[:, h * bq:(h + 1) * bq], wuv_ref[h]) for h in range(H_A)]
    o_ref[0] = jnp.concatenate(outs, axis=1).astype(_BF16)


def _dsa(qa, iq, iwt, ik, ckv, wukt, wuv, bias_tbl, tri, *, k_top, kc):
    bsz, seq, _ = qa.shape
    bq = BLOCK_Q
    q_spec = pl.BlockSpec((1, bq, H_A * DH_A), lambda b, i: (b, i, 0))
    return pl.pallas_call(
        functools.partial(_dsa_kernel, k_top=k_top, kc=kc),
        out_shape=jax.ShapeDtypeStruct((bsz, seq, H_A * DH_A), _BF16),
        grid=(bsz, seq // bq),
        in_specs=[q_spec, q_spec,
                  pl.BlockSpec((1, H_IDX, bq), lambda b, i: (b, 0, i)),
                  pl.BlockSpec((1, seq, DH_IDX), lambda b, i: (b, 0, 0)),
                  pl.BlockSpec((1, seq, DC_A), lambda b, i: (b, 0, 0)),
                  _const_spec(wukt.shape), _const_spec(wuv.shape),
                  _const_spec(bias_tbl.shape), _const_spec(tri.shape)],
        out_specs=q_spec,
        scratch_shapes=[pltpu.VMEM((seq, bq), _I32),
                        pltpu.VMEM((kc, H_A * bq), _BF16),
                        pltpu.VMEM((DC_A, H_A * bq), _F32),
                        pltpu.VMEM((1, H_A * bq), _F32),
                        pltpu.VMEM((1, H_A * bq), _F32)],
        compiler_params=_params(2),
        name="dsa",
    )(qa, iq, iwt, ik, ckv, wukt, wuv, bias_tbl, tri)


def _sb_kernel(q_ref, k_ref, v_ref, tri2_ref, o_ref, acc_ref, run_ref):
    bq = BLOCK_Q
    i = pl.program_id(1)
    q = q_ref[0]
    scale = DH_SB ** -0.5
    run_ref[...] = jnp.zeros(run_ref.shape, _F32)
    acc_ref[...] = jnp.zeros(acc_ref.shape, _F32)
    row = lax.broadcasted_iota(_I32, (bq, bq), 0)
    col = lax.broadcasted_iota(_I32, (bq, bq), 1)

    def cond(state):
        j, run_max = state
        return (j >= 0) & (run_max > _SB_DEAD_LOG)

    def body(state):
        j, _ = state
        s0 = pl.multiple_of(j * bq, bq)
        kb = k_ref[0, pl.ds(s0, bq), :]
        vb = v_ref[0, pl.ds(s0, bq), :]
        strict = row + (j - i) * bq < col
        zs, log_nots = [], []
        for h in range(H_SB):
            hs = slice(h * DH_SB, (h + 1) * DH_SB)
            z = _dot_nt(kb[:, hs], q[:, hs]) * scale
            sp = jnp.maximum(z, 0.0) + jnp.log1p(jnp.exp(-jnp.abs(z)))
            zs.append((z, sp))
            log_nots.append(jnp.where(strict, -sp, 0.0))
        log_not = jnp.concatenate(log_nots, axis=1)
        hi = log_not.astype(_BF16)
        lo = (log_not - hi.astype(_F32)).astype(_BF16)
        later = _dot(tri2_ref[...], jnp.concatenate([hi, lo], axis=0)) + run_ref[...]
        outs = []
        for h in range(H_SB):
            z, sp = zs[h]
            a = jnp.where(strict, jnp.exp(z - sp + later[:, h * bq:(h + 1) * bq]), 0.0)
            outs.append(_dot_tn(vb[:, h * DH_SB:(h + 1) * DH_SB], a.astype(_BF16)))
        acc_ref[...] += jnp.concatenate(outs, axis=0)
        run = run_ref[...] + jnp.sum(log_not, axis=0, keepdims=True)
        run_ref[...] = run
        return j - 1, jnp.max(run)

    lax.while_loop(cond, body, (i, jnp.float32(0.0)))
    o_ref[0] = acc_ref[...].T.astype(_BF16)


def _stick_breaking(q, k, v, tri2):
    bsz, seq, width = q.shape
    bq = BLOCK_Q
    return pl.pallas_call(
        _sb_kernel,
        out_shape=jax.ShapeDtypeStruct((bsz, seq, width), _BF16),
        grid=(bsz, seq // bq),
        in_specs=[pl.BlockSpec((1, bq, width), lambda b, i: (b, i, 0)),
                  pl.BlockSpec((1, seq, width), lambda b, i: (b, 0, 0)),
                  pl.BlockSpec((1, seq, width), lambda b, i: (b, 0, 0)),
                  _const_spec(tri2.shape)],
        out_specs=pl.BlockSpec((1, bq, width), lambda b, i: (b, i, 0)),
        scratch_shapes=[pltpu.VMEM((width, bq), _F32), pltpu.VMEM((1, H_SB * bq), _F32)],
        compiler_params=_params(2),
        name="stick_breaking",
    )(q, k, v, tri2)


def _merge_ln_kernel(x_ref, ya_ref, yb_ref, wcg_ref, convw_ref, bgate_ref, wbr_ref, wout_ref,
                     g_ref, b_ref, o_ref, halo_ref, *, alpha, tiles_per_seq):
    i = pl.program_id(0)
    x = x_ref[...]
    xb = x.astype(_BF16)
    tm = x.shape[0]
    c = C_CONV

    @pl.when(i % tiles_per_seq == 0)
    def _():
        halo_ref[...] = jnp.zeros(halo_ref.shape, _F32)

    cb = _dot(xb, wcg_ref[:, 0:c])
    z = _dot(xb, wcg_ref[:, c:2 * c]) * _dot(xb, wcg_ref[:, 2 * c:3 * c])
    halo = halo_ref[...]
    row = lax.broadcasted_iota(_I32, (tm, c), 0)
    last = V7X_SUBLANES - 1
    z1 = jnp.where(row == 0, halo[last:last + 1, :], pltpu.roll(z, 1, 0))
    z2 = jnp.where(row == 0, halo[last - 1:last, :],
                   jnp.where(row == 1, halo[last:last + 1, :], pltpu.roll(z, 2, 0)))
    halo_ref[...] = z[tm - V7X_SUBLANES:, :]
    y_c = cb * (convw_ref[0:1, :] * z2 + convw_ref[1:2, :] * z1 + convw_ref[2:3, :] * z)

    d = x.shape[1]
    merged = jnp.zeros((tm, d), _F32)
    branches = (ya_ref[...], yb_ref[...], y_c.astype(_BF16))
    for n, y in enumerate(branches):
        gate = _dot(xb, wcg_ref[:, 3 * c + n * d:3 * c + (n + 1) * d]) + bgate_ref[n:n + 1, :]
        merged = merged + jax.nn.sigmoid(gate) * _dot(y, wbr_ref[n])
    y = alpha * x + _dot(merged.astype(_BF16), wout_ref[...])
    o_ref[...] = _layer_norm(y, g_ref[...], b_ref[...])


def _merge_ln(x, ya, yb, wcg, convw, bgate, wbr, wout, g, b, *, alpha, tm, seq):
    t, d = x.shape
    return pl.pallas_call(
        functools.partial(_merge_ln_kernel, alpha=alpha, tiles_per_seq=seq // tm),
        out_shape=jax.ShapeDtypeStruct((t, d), _F32),
        grid=(t // tm,),
        in_specs=[pl.BlockSpec((tm, d), lambda i: (i, 0)),
                  pl.BlockSpec((tm, W_BRANCH), lambda i: (i, 0)),
                  pl.BlockSpec((tm, W_BRANCH), lambda i: (i, 0)),
                  _const_spec(wcg.shape), _const_spec(convw.shape), _const_spec(bgate.shape),
                  _const_spec(wbr.shape), _const_spec(wout.shape),
                  _const_spec((1, d)), _const_spec((1, d))],
        out_specs=pl.BlockSpec((tm, d), lambda i: (i, 0)),
        scratch_shapes=[pltpu.VMEM((V7X_SUBLANES, C_CONV), _F32)],
        compiler_params=_params(1),
        name="merge_ln",
    )(x, ya, yb, wcg, convw, bgate, wbr, wout, g, b)


def _mem_kv_kernel(mem_ref, wkv_ref, k_ref, v_ref):
    kv = _dot(mem_ref[0].astype(_BF16), wkv_ref[...])
    w = H_X * DH_X
    k_ref[0] = kv[:, :w].astype(_BF16)
    v_ref[0] = kv[:, w:].astype(_BF16)


def _mem_kv(mem, wkv):
    bsz, m, d = mem.shape
    w = H_X * DH_X
    spec = pl.BlockSpec((1, m, w), lambda b: (b, 0, 0))
    return pl.pallas_call(
        _mem_kv_kernel,
        out_shape=[jax.ShapeDtypeStruct((bsz, m, w), _BF16)] * 2,
        grid=(bsz,),
        in_specs=[pl.BlockSpec((1, m, d), lambda b: (b, 0, 0)), _const_spec(wkv.shape)],
        out_specs=[spec, spec],
        compiler_params=_params(1),
        name="mem_kv",
    )(mem, wkv)


def _xattn_ln_kernel(x_ref, k_ref, v_ref, wq_ref, wo_ref, g_ref, b_ref, o_ref, *, alpha):
    x = x_ref[...]
    q = _dot(x.astype(_BF16), wq_ref[...]).astype(_BF16)
    k = k_ref[0]
    v = v_ref[0]
    outs = []
    for h in range(H_X):
        hs = slice(h * DH_X, (h + 1) * DH_X)
        lt = _dot_nt(k[:, hs], q[:, hs]) * (DH_X ** -0.5)
        p = jnp.exp(lt - jnp.max(lt, axis=0, keepdims=True))
        p = p * (1.0 / jnp.sum(p, axis=0, keepdims=True))
        outs.append(_dot_tn(p.astype(_BF16), v[:, hs]))
    o = jnp.concatenate(outs, axis=1).astype(_BF16)
    y = alpha * x + _dot(o, wo_ref[...])
    o_ref[...] = _layer_norm(y, g_ref[...], b_ref[...])


def _xattn_ln(x, k, v, wq, wo, g, b, *, alpha, tm, seq):
    t, d = x.shape
    m, w = k.shape[1], k.shape[2]
    tiles_per_seq = seq // tm
    kv_spec = pl.BlockSpec((1, m, w), lambda i: (i // tiles_per_seq, 0, 0))
    return pl.pallas_call(
        functools.partial(_xattn_ln_kernel, alpha=alpha),
        out_shape=jax.ShapeDtypeStruct((t, d), _F32),
        grid=(t // tm,),
        in_specs=[pl.BlockSpec((tm, d), lambda i: (i, 0)), kv_spec, kv_spec,
                  _const_spec(wq.shape), _const_spec(wo.shape),
                  _const_spec((1, d)), _const_spec((1, d))],
        out_specs=pl.BlockSpec((tm, d), lambda i: (i, 0)),
        compiler_params=_params(1),
        name="xattn_ln",
    )(x, k, v, wq, wo, g, b)


def _t5_bucket(n):
    max_exact = N_BUCKETS // 2
    n = jnp.maximum(n, 0)
    nf = jnp.maximum(n, 1).astype(_F32)
    large = max_exact + (jnp.log(nf / max_exact) / math.log(MAX_DISTANCE / max_exact)
                         * (N_BUCKETS - max_exact)).astype(_I32)
    large = jnp.minimum(large, N_BUCKETS - 1)
    return jnp.where(n < max_exact, n, large)


def _bias_tables(rel_bias):
    bq = BLOCK_Q
    s = jnp.arange(bq, dtype=_I32)[:, None]
    t = jnp.arange(bq, dtype=_I32)[None, :]
    dist = jnp.stack([d * bq + t - s for d in range(3)])
    tbl = rel_bias[_t5_bucket(dist)]
    return jnp.transpose(tbl, (0, 3, 1, 2)).astype(_F32)


def _strict_upper(n):
    r = np.arange(n)
    return (r[None, :] > r[:, None])


def kernel(x, mem, ln_g, ln_b, ffn_w_in, ffn_w_out, w_mix_in, b_gate, kv_norm_g, w_uk, w_uv, conv_w,
           w_branch, w_mix_out, xa_wq, xa_wkv, xa_wo, rel_bias):
    bsz, seq, d = x.shape
    depth = ln_g.shape[0]
    d_ff = ffn_w_out.shape[2]
    t = bsz * seq
    alpha = (2 * depth) ** 0.25
    k_top = min(TOPK_MAX, seq // 4)
    bq = BLOCK_Q
    kc = min(4 * bq, seq)
    tm = min(512, seq)
    tf = 256

    offs = np.cumsum((H_A * DH_A, DC_A, H_IDX * DH_IDX, DH_IDX, H_IDX, W_BRANCH, W_BRANCH, W_BRANCH,
                      C_CONV, C_CONV, C_CONV, d, d, d))
    o_qa, o_ckv, o_iq, o_ik, o_iw, o_qs, o_ks, o_vs, o_cb = (0,) + tuple(int(o) for o in offs[:8])

    bias_tbl = _bias_tables(rel_bias)
    tri = jnp.asarray(_strict_upper(kc).T, _BF16)
    up = _strict_upper(bq)
    tri2 = jnp.asarray(np.concatenate([up, up], axis=1), _BF16)

    x = x.reshape(t, d)
    for l in range(depth):
        g = ln_g[l][:, None, :]
        b = ln_b[l][:, None, :]
        bf = lambda a: a.astype(_BF16)

        wa, wb = ffn_w_in[l, 0, :, :d_ff], ffn_w_in[l, 0, :, d_ff:]
        x = _ffn_ln(x, bf(wa), bf(wb), bf(ffn_w_out[l, 0]), g[0], b[0], alpha=alpha, tm=tm, tf=tf)

        wm = w_mix_in[l]
        pad = jnp.zeros((d, V7X_LANES - DH_IDX - H_IDX), wm.dtype)
        w_proj = jnp.concatenate(
            [wm[:, o_qa:o_ckv], wm[:, o_iq:o_ik], wm[:, o_qs:o_ks], wm[:, o_ks:o_vs], wm[:, o_vs:o_cb],
             wm[:, o_ckv:o_iq], wm[:, o_ik:o_qs], pad], axis=1)
        qa, iq, qs, ks, vs, ckv, ikw = _mix_proj(x, bf(w_proj), kv_norm_g[l][None, :], tm=tm)
        ikw = ikw.reshape(bsz, seq, V7X_LANES)
        ik = ikw[:, :, :DH_IDX].astype(_BF16)
        iwt = jnp.swapaxes(ikw[:, :, DH_IDX:DH_IDX + H_IDX], 1, 2)
        r3 = lambda a: a.reshape(bsz, seq, a.shape[-1])
        y_a = _dsa(r3(qa), r3(iq), iwt, ik, r3(ckv), bf(jnp.swapaxes(w_uk[l], 1, 2)), bf(w_uv[l]),
                   bias_tbl, tri, k_top=k_top, kc=kc)
        y_b = _stick_breaking(r3(qs), r3(ks), r3(vs), tri2)
        x = _merge_ln(x, y_a.reshape(t, -1), y_b.reshape(t, -1), bf(wm[:, o_cb:]), conv_w[l], b_gate[l],
                      bf(w_branch[l]), bf(w_mix_out[l]), g[1], b[1], alpha=alpha, tm=tm, seq=seq)

        k_mem, v_mem = _mem_kv(mem, bf(xa_wkv[l]))
        x = _xattn_ln(x, k_mem, v_mem, bf(xa_wq[l]), bf(xa_wo[l]), g[2], b[2], alpha=alpha, tm=tm, seq=seq)

        wa, wb = ffn_w_in[l, 1, :, :d_ff], ffn_w_in[l, 1, :, d_ff:]
        x = _ffn_ln(x, bf(wa), bf(wb), bf(ffn_w_out[l, 1]), g[3], b[3], alpha=alpha, tm=tm, tf=tf)
    return x.reshape(bsz, seq, d)
```

```python
import functools
import math

import jax
import jax.numpy as jnp
import numpy as np
from jax import lax
from jax.experimental import pallas as pl
from jax.experimental.pallas import tpu as pltpu

BLOCK_Q = 128
H_A, DH_A, DC_A = 8, 64, 128
DC_AUG = DC_A + 16
H_IDX, DH_IDX = 8, 64
TOPK_MAX = 256
H_SB, DH_SB = 8, 64
C_CONV, CONV_W = 512, 3
W_BRANCH = 512
H_X, DH_X = 4, 128
N_BUCKETS, MAX_DISTANCE = 32, 128
LN_EPS = 1e-5

V7X_LANES = 128
V7X_SUBLANES = 8
V7X_VMEM_LIMIT_BYTES = 56 * 1024 * 1024

_F32 = jnp.float32
_BF16 = jnp.bfloat16
_I32 = jnp.int32

_INT_MIN = -(2 ** 31)
_NEG_INF_KEY = np.int32(np.uint32(0x807FFFFF).view(np.int32))
_MASK_VALUE = -1e30
_LOG2E = math.log2(math.e)
_SB_DEAD_LOG = -160.0
_SB_GROUP = 8


def _dot(a, b):
    return jnp.dot(a, b, preferred_element_type=_F32)


def _dot_nt(a, b):
    return lax.dot_general(a, b, (((1,), (1,)), ((), ())), preferred_element_type=_F32)


def _dot_tn(a, b):
    return lax.dot_general(a, b, (((0,), (0,)), ((), ())), preferred_element_type=_F32)


def _layer_norm(y, g, b):
    mu = jnp.mean(y, axis=-1, keepdims=True)
    yc = y - mu
    var = jnp.mean(yc * yc, axis=-1, keepdims=True)
    return yc * lax.rsqrt(var + LN_EPS) * g + b


def _params(n_grid, **kw):
    return pltpu.CompilerParams(
        dimension_semantics=("arbitrary",) * n_grid,
        vmem_limit_bytes=V7X_VMEM_LIMIT_BYTES, **kw)


def _const_spec(shape):
    n = len(shape)
    return pl.BlockSpec(shape, lambda *_: (0,) * n, pipeline_mode=pl.Buffered(1))


def _ffn_ln_kernel(x_ref, wa_ref, wb_ref, wo_ref, g_ref, b_ref, o_ref, *, alpha, tf):
    x = x_ref[...]
    xb = x.astype(_BF16)
    d_ff = wa_ref.shape[1]
    acc = jnp.zeros(x.shape, _F32)
    for c in range(d_ff // tf):
        sl = slice(c * tf, (c + 1) * tf)
        a = _dot(xb, wa_ref[:, sl])
        b = _dot(xb, wb_ref[:, sl])
        h = (a * jax.nn.sigmoid(a) * b).astype(_BF16)
        acc = acc + _dot(h, wo_ref[sl, :])
    y = alpha * x + 0.5 * acc
    o_ref[...] = _layer_norm(y, g_ref[...], b_ref[...])


def _ffn_ln(x, wa, wb, wo, g, b, *, alpha, tm, tf):
    t, d = x.shape
    f = wa.shape[1]
    return pl.pallas_call(
        functools.partial(_ffn_ln_kernel, alpha=alpha, tf=tf),
        out_shape=jax.ShapeDtypeStruct((t, d), _F32),
        grid=(t // tm,),
        in_specs=[pl.BlockSpec((tm, d), lambda i: (i, 0)),
                  _const_spec((d, f)), _const_spec((d, f)), _const_spec((f, d)),
                  _const_spec((1, d)), _const_spec((1, d))],
        out_specs=pl.BlockSpec((tm, d), lambda i: (i, 0)),
        compiler_params=_params(1),
        name="ffn_ln",
    )(x, wa, wb, wo, g, b)


def _mix_proj_kernel(x_ref, w_ref, kvg_ref, qa_ref, iq_ref, qs_ref, ks_ref, vs_ref, ckv_ref, ckvt_ref,
                     ikw_ref):
    xb = x_ref[...].astype(_BF16)
    wide = W_BRANCH
    for j, ref in enumerate((qa_ref, iq_ref, qs_ref, ks_ref, vs_ref)):
        ref[...] = _dot(xb, w_ref[:, j * wide:(j + 1) * wide]).astype(_BF16)
    off = 5 * wide
    ckv = _dot(xb, w_ref[:, off:off + DC_A])
    ckv = ckv * lax.rsqrt(jnp.mean(ckv * ckv, axis=-1, keepdims=True) + LN_EPS) * kvg_ref[...]
    ckv_ref[...] = ckv.astype(_BF16)
    ones = jnp.ones((DC_AUG - DC_A, ckv.shape[0]), _F32)
    ckvt_ref[...] = jnp.concatenate([ckv.T, ones], axis=0).astype(_BF16)
    ikw_ref[...] = _dot(xb, w_ref[:, off + DC_A:off + DC_A + V7X_LANES])


def _mix_proj(x, w, kvg, *, tm):
    t, d = x.shape
    n = w.shape[1]
    wide_spec = pl.BlockSpec((tm, W_BRANCH), lambda i: (i, 0))
    lane_spec = pl.BlockSpec((tm, V7X_LANES), lambda i: (i, 0))
    return pl.pallas_call(
        _mix_proj_kernel,
        out_shape=[jax.ShapeDtypeStruct((t, W_BRANCH), _BF16)] * 5
        + [jax.ShapeDtypeStruct((t, DC_A), _BF16), jax.ShapeDtypeStruct((DC_AUG, t), _BF16),
           jax.ShapeDtypeStruct((t, V7X_LANES), _F32)],
        grid=(t // tm,),
        in_specs=[pl.BlockSpec((tm, d), lambda i: (i, 0)), _const_spec((d, n)), _const_spec((1, DC_A))],
        out_specs=[wide_spec] * 5 + [lane_spec, pl.BlockSpec((DC_AUG, tm), lambda i: (0, i)), lane_spec],
        compiler_params=_params(1),
        name="mix_proj",
    )(x, w, kvg)


def _bit_transpose32(words):
    a = list(words)
    j, m = 16, 0x0000FFFF
    while j:
        mask = np.int32(np.uint32(m).view(np.int32))
        for k in range(32):
            if k & j == 0:
                t = ((a[k] >> j) ^ a[k + j]) & mask
                a[k] = a[k] ^ (t << j)
                a[k + j] = a[k + j] ^ t
        j >>= 1
        m ^= (m << j) & 0xFFFFFFFF
    return a


def _dsa_kernel(qa_ref, iq_ref, iwt_ref, ik_ref, ckv_ref, ckvt_ref, wukt_ref, wuv_ref, bucket_ref,
                relb_ref, tri_ref, o_ref, keys_ref, planes_ref, bias_ref, qlat_ref, lt_ref, alpha_ref,
                mnew_ref, acc_ref, m_ref, *, k_top, kc):
    bq = BLOCK_Q
    b_idx = pl.program_id(0)
    i = pl.program_id(1)
    q0 = i * bq
    n_chunks = (q0 + bq + kc - 1) // kc
    sub = kc // bq
    groups = kc // 256
    words_c = kc // 32
    n_words = planes_ref.shape[1]

    @pl.when((b_idx == 0) & (i == 0))
    def _():
        planes_ref[...] = jnp.zeros(planes_ref.shape, _I32)
        bias_ref[...] = jnp.zeros(bias_ref.shape, _F32)

        def fill(bkt, carry):
            hit = bucket_ref[...] == bkt
            for h in range(H_A):
                bias_ref[:, h] = jnp.where(hit, relb_ref[bkt, h] * _LOG2E, bias_ref[:, h])
            return carry

        lax.fori_loop(0, N_BUCKETS, fill, 0)

    qa = qa_ref[0]
    iq = iq_ref[0]
    iq_rows = jnp.concatenate([iq[:, h * DH_IDX:(h + 1) * DH_IDX] for h in range(H_IDX)], axis=0)
    w_idx = iwt_ref[0] * ((H_IDX ** -0.5) * (DH_IDX ** -0.5))
    t_idx = q0 + lax.broadcasted_iota(_I32, (1, bq), 1)

    def score_chunk(c, carry):
        s0 = pl.multiple_of(c * kc, kc)
        d = _dot_nt(ik_ref[0, pl.ds(s0, kc), :], iq_rows)
        sc = jnp.zeros((kc, bq), _F32)
        for h in range(H_IDX):
            sc = sc + jnp.maximum(d[:, h * bq:(h + 1) * bq], 0.0) * w_idx[h:h + 1, :]
        s_idx = s0 + lax.broadcasted_iota(_I32, (kc, bq), 0)
        sc = jnp.where(s_idx <= t_idx, sc, -jnp.inf)
        bits = pltpu.bitcast(sc, _I32)
        keys = bits ^ ((bits >> 31) & 0x7FFFFFFF)
        keys_ref[pl.ds(s0, kc), :] = keys
        u = (keys ^ _INT_MIN).reshape(groups, 32, V7X_SUBLANES, bq)
        planes = _bit_transpose32([u[:, j] for j in range(32)])
        w0 = pl.multiple_of(c * words_c, words_c)
        for b in range(32):
            planes_ref[b, pl.ds(w0, words_c), :] = planes[b].reshape(words_c, bq)
        return carry

    lax.fori_loop(0, n_chunks, score_chunk, 0)

    word_row = lax.broadcasted_iota(_I32, (n_words, bq), 0)
    alive0 = jnp.where(word_row < n_chunks * words_c, -1, 0)

    def bit_step(it, state):
        alive, need, thr_u = state
        b = 31 - it
        ones = alive & planes_ref[b]
        cnt = lax.population_count(ones)
        cnt = jnp.sum(cnt.reshape(n_words // V7X_SUBLANES, V7X_SUBLANES, bq), axis=0)
        cnt = jnp.sum(cnt, axis=0, keepdims=True)
        take = cnt >= need
        alive = jnp.where(take, ones, alive ^ ones)
        need = jnp.where(take, need, need - cnt)
        thr_u = thr_u | jnp.where(take, lax.shift_left(jnp.int32(1), b), 0)
        return alive, need, thr_u

    def sublane_fold(a, op):
        return op(a.reshape(a.shape[0] // V7X_SUBLANES, V7X_SUBLANES, a.shape[1]), axis=0)

    alive, need, thr_u = lax.fori_loop(
        0, 32, bit_step, (alive0, jnp.full((1, bq), k_top, _I32), jnp.zeros((1, bq), _I32)))
    thr = thr_u ^ _INT_MIN
    n_ties = jnp.where(thr > _NEG_INF_KEY, need, 0)
    thr_gt = jnp.maximum(thr, _NEG_INF_KEY)
    n_equal = jnp.sum(sublane_fold(lax.population_count(alive), jnp.sum), axis=0, keepdims=True)
    ranked = jnp.max(jnp.where(thr > _NEG_INF_KEY, n_equal - need, 0)) > 0

    for h in range(H_A):
        q_h = _dot_nt(wukt_ref[h], qa[:, h * DH_A:(h + 1) * DH_A]) * ((DH_A ** -0.5) * _LOG2E)
        qlat_ref[:, h * bq:(h + 1) * bq] = q_h.astype(_BF16)

    m_ref[...] = jnp.full(m_ref.shape, -jnp.inf, _F32)
    acc_ref[...] = jnp.zeros(acc_ref.shape, _F32)

    def stage_logits(c, slot, ties_before, use_rank):
        s0 = pl.multiple_of(c * kc, kc)
        keys = keys_ref[pl.ds(s0, kc), :]
        if use_rank:
            eq = keys == thr
            eq_f = jnp.where(eq, 1.0, 0.0)
            rank = ties_before + _dot(tri_ref[...], eq_f.astype(_BF16))
            sel = (keys > thr_gt) | (eq & (rank < n_ties.astype(_F32)))
            ties_before = ties_before + jnp.sum(sublane_fold(eq_f, jnp.sum), axis=0, keepdims=True)
        else:
            sel = keys >= jnp.maximum(thr, _NEG_INF_KEY + 1)
        mask_add = jnp.where(sel, 0.0, _MASK_VALUE)

        ckv_c = ckv_ref[0, pl.ds(s0, kc), :]
        d_blk = [jnp.clip(i - (c * sub + j), 0, 2) for j in range(sub)]
        for hp in range(H_A // 2):
            logits = _dot(ckv_c, qlat_ref[:, hp * 2 * bq:(hp + 1) * 2 * bq])
            for hh in range(2):
                h = 2 * hp + hh
                hs = slice(h * bq, (h + 1) * bq)
                m_fold = None
                for j in range(sub):
                    rs = slice(j * bq, (j + 1) * bq)
                    lt = logits[rs, hh * bq:(hh + 1) * bq] + bias_ref[d_blk[j], h] + mask_add[rs, :]
                    lt_ref[slot, rs, hs] = lt
                    lt_fold = sublane_fold(lt, jnp.max)
                    m_fold = lt_fold if m_fold is None else jnp.maximum(m_fold, lt_fold)
                m_old = m_ref[:, hs]
                m_new = jnp.maximum(m_old, jnp.max(m_fold, axis=0, keepdims=True))
                alpha_ref[slot, :, hs] = jnp.exp2(m_old - m_new)
                mnew_ref[slot, :, hs] = m_new
                m_ref[:, hs] = m_new
        return ties_before

    def stage_accumulate(c, slot):
        s0 = pl.multiple_of(c * kc, kc)
        ckvt_c = ckvt_ref[:, pl.ds(s0, kc)]
        for hp in range(H_A // 2):
            cs = slice(hp * 2 * bq, (hp + 1) * 2 * bq)
            p = jnp.exp2(lt_ref[slot, :, cs] - mnew_ref[slot, :, cs])
            acc_ref[:, cs] = acc_ref[:, cs] * alpha_ref[slot, :, cs] + _dot(ckvt_c, p.astype(_BF16))

    def attend(use_rank):
        def chunk_pair(k, ties):
            c = 2 * k
            ties = stage_logits(c + 1, 1, ties, use_rank)
            stage_accumulate(c, 0)
            ties = stage_logits(c + 2, 0, ties, use_rank)
            stage_accumulate(c + 1, 1)
            return ties

        ties = stage_logits(0, 0, jnp.zeros((1, bq), _F32), use_rank)
        n_pairs = (n_chunks - 1) // 2
        ties = lax.fori_loop(0, n_pairs, chunk_pair, ties)
        c_tail = 2 * n_pairs
        even_tail = n_chunks - c_tail == 2

        @pl.when(even_tail)
        def _():
            stage_logits(c_tail + 1, 1, ties, use_rank)

        stage_accumulate(c_tail, 0)

        @pl.when(even_tail)
        def _():
            stage_accumulate(c_tail + 1, 1)

    lax.cond(ranked, lambda: attend(True), lambda: attend(False))

    inv_l = 1.0 / acc_ref[DC_A:DC_A + 1, :]
    o_lat_t = (acc_ref[0:DC_A, :] * inv_l).astype(_BF16)
    outs = [_dot_tn(o_lat_t[:, h * bq:(h + 1) * bq], wuv_ref[h]) for h in range(H_A)]
    o_ref[0] = jnp.concatenate(outs, axis=1).astype(_BF16)


def _dsa(qa, iq, iwt, ik, ckv, ckvt, wukt, wuv, bucket_tbl, rel_bias, tri, *, k_top, kc):
    bsz, seq, _ = qa.shape
    bq = BLOCK_Q
    q_spec = pl.BlockSpec((1, bq, H_A * DH_A), lambda b, i: (b, i, 0))
    return pl.pallas_call(
        functools.partial(_dsa_kernel, k_top=k_top, kc=kc),
        out_shape=jax.ShapeDtypeStruct((bsz, seq, H_A * DH_A), _BF16),
        grid=(bsz, seq // bq),
        in_specs=[q_spec, q_spec,
                  pl.BlockSpec((1, H_IDX, bq), lambda b, i: (b, 0, i)),
                  pl.BlockSpec((1, seq, DH_IDX), lambda b, i: (b, 0, 0)),
                  pl.BlockSpec((1, seq, DC_A), lambda b, i: (b, 0, 0)),
                  pl.BlockSpec((DC_AUG, seq), lambda b, i: (0, b)),
                  _const_spec(wukt.shape), _const_spec(wuv.shape), _const_spec(bucket_tbl.shape),
                  pl.BlockSpec(memory_space=pltpu.SMEM),
                  _const_spec(tri.shape)],
        out_specs=q_spec,
        scratch_shapes=[pltpu.VMEM((seq, bq), _I32),
                        pltpu.VMEM((32, seq // 32, bq), _I32),
                        pltpu.VMEM((3, H_A, bq, bq), _F32),
                        pltpu.VMEM((DC_A, H_A * bq), _BF16),
                        pltpu.VMEM((2, kc, H_A * bq), _F32),
                        pltpu.VMEM((2, 1, H_A * bq), _F32),
                        pltpu.VMEM((2, 1, H_A * bq), _F32),
                        pltpu.VMEM((DC_AUG, H_A * bq), _F32),
                        pltpu.VMEM((1, H_A * bq), _F32)],
        compiler_params=_params(2),
        name="dsa",
    )(qa, iq, iwt, ik, ckv, ckvt, wukt, wuv, bucket_tbl, rel_bias, tri)


def _sb_kernel(q_ref, k_ref, v_ref, tri2_ref, o_ref, acc_ref, run_ref):
    bq = BLOCK_Q
    i = pl.program_id(1)
    q = q_ref[0]
    scale = DH_SB ** -0.5
    run_ref[...] = jnp.zeros(run_ref.shape, _F32)
    acc_ref[...] = jnp.zeros(acc_ref.shape, _F32)
    row = lax.broadcasted_iota(_I32, (bq, bq), 0)
    col = lax.broadcasted_iota(_I32, (bq, bq), 1)

    def cond(state):
        j, run_max = state
        return (j >= 0) & (run_max > _SB_DEAD_LOG)

    def body(state):
        j, _ = state
        s0 = pl.multiple_of(j * bq, bq)
        kb = k_ref[0, pl.ds(s0, bq), :]
        vb = v_ref[0, pl.ds(s0, bq), :]
        strict = row + (j - i) * bq < col
        run_max = None
        for grp in range(H_SB // _SB_GROUP):
            heads = range(grp * _SB_GROUP, (grp + 1) * _SB_GROUP)
            ys, log_nots = [], []
            for h in heads:
                hs = slice(h * DH_SB, (h + 1) * DH_SB)
                y = _dot_nt(kb[:, hs], q[:, hs]) * (-scale * _LOG2E)
                y = jnp.where(strict, y, -_MASK_VALUE)
                ys.append(y)
                log_nots.append(jnp.minimum(y, 0.0) - jnp.log2(1.0 + jnp.exp2(-jnp.abs(y))))
            log_not = jnp.concatenate(log_nots, axis=1)
            hi = log_not.astype(_BF16)
            lo = (log_not - hi.astype(_F32)).astype(_BF16)
            gs = slice(grp * _SB_GROUP * bq, (grp + 1) * _SB_GROUP * bq)
            later = _dot(tri2_ref[...], jnp.concatenate([hi, lo], axis=0)) + run_ref[:, gs]
            outs = []
            for n, h in enumerate(heads):
                a = jnp.exp2(log_nots[n] - ys[n] + later[:, n * bq:(n + 1) * bq])
                outs.append(_dot_tn(vb[:, h * DH_SB:(h + 1) * DH_SB], a.astype(_BF16)))
            ds = slice(grp * _SB_GROUP * DH_SB, (grp + 1) * _SB_GROUP * DH_SB)
            acc_ref[ds, :] += jnp.concatenate(outs, axis=0)
            fold = jnp.sum(log_not.reshape(bq // V7X_SUBLANES, V7X_SUBLANES, log_not.shape[1]), axis=0)
            run = run_ref[:, gs] + jnp.sum(fold, axis=0, keepdims=True)
            run_ref[:, gs] = run
            run_max = jnp.max(run) if run_max is None else jnp.maximum(run_max, jnp.max(run))
        return j - 1, run_max

    lax.while_loop(cond, body, (i, jnp.float32(0.0)))
    o_ref[0] = acc_ref[...].T.astype(_BF16)


def _stick_breaking(q, k, v, tri2):
    bsz, seq, width = q.shape
    bq = BLOCK_Q
    return pl.pallas_call(
        _sb_kernel,
        out_shape=jax.ShapeDtypeStruct((bsz, seq, width), _BF16),
        grid=(bsz, seq // bq),
        in_specs=[pl.BlockSpec((1, bq, width), lambda b, i: (b, i, 0)),
                  pl.BlockSpec((1, seq, width), lambda b, i: (b, 0, 0)),
                  pl.BlockSpec((1, seq, width), lambda b, i: (b, 0, 0)),
                  _const_spec(tri2.shape)],
        out_specs=pl.BlockSpec((1, bq, width), lambda b, i: (b, i, 0)),
        scratch_shapes=[pltpu.VMEM((width, bq), _F32), pltpu.VMEM((1, H_SB * bq), _F32)],
        compiler_params=_params(2),
        name="stick_breaking",
    )(q, k, v, tri2)


def _merge_ln_kernel(x_ref, ya_ref, yb_ref, wcg_ref, convw_ref, bgate_ref, wbr_ref, wout_ref,
                     g_ref, b_ref, o_ref, halo_ref, *, alpha, tiles_per_seq):
    i = pl.program_id(0)
    x = x_ref[...]
    xb = x.astype(_BF16)
    tm = x.shape[0]
    c = C_CONV

    @pl.when(i % tiles_per_seq == 0)
    def _():
        halo_ref[...] = jnp.zeros(halo_ref.shape, _F32)

    cb = _dot(xb, wcg_ref[:, 0:c])
    z = _dot(xb, wcg_ref[:, c:2 * c]) * _dot(xb, wcg_ref[:, 2 * c:3 * c])
    halo = halo_ref[...]
    row = lax.broadcasted_iota(_I32, (tm, c), 0)
    last = V7X_SUBLANES - 1
    z1 = jnp.where(row == 0, halo[last:last + 1, :], pltpu.roll(z, 1, 0))
    z2 = jnp.where(row == 0, halo[last - 1:last, :],
                   jnp.where(row == 1, halo[last:last + 1, :], pltpu.roll(z, 2, 0)))
    halo_ref[...] = z[tm - V7X_SUBLANES:, :]
    y_c = cb * (convw_ref[0:1, :] * z2 + convw_ref[1:2, :] * z1 + convw_ref[2:3, :] * z)

    d = x.shape[1]
    merged = jnp.zeros((tm, d), _F32)
    branches = (ya_ref[...], yb_ref[...], y_c.astype(_BF16))
    for n, y in enumerate(branches):
        gate = _dot(xb, wcg_ref[:, 3 * c + n * d:3 * c + (n + 1) * d]) + bgate_ref[n:n + 1, :]
        merged = merged + jax.nn.sigmoid(gate) * _dot(y, wbr_ref[n])
    y = alpha * x + _dot(merged.astype(_BF16), wout_ref[...])
    o_ref[...] = _layer_norm(y, g_ref[...], b_ref[...])


def _merge_ln(x, ya, yb, wcg, convw, bgate, wbr, wout, g, b, *, alpha, tm, seq):
    t, d = x.shape
    return pl.pallas_call(
        functools.partial(_merge_ln_kernel, alpha=alpha, tiles_per_seq=seq // tm),
        out_shape=jax.ShapeDtypeStruct((t, d), _F32),
        grid=(t // tm,),
        in_specs=[pl.BlockSpec((tm, d), lambda i: (i, 0)),
                  pl.BlockSpec((tm, W_BRANCH), lambda i: (i, 0)),
                  pl.BlockSpec((tm, W_BRANCH), lambda i: (i, 0)),
                  _const_spec(wcg.shape), _const_spec(convw.shape), _const_spec(bgate.shape),
                  _const_spec(wbr.shape), _const_spec(wout.shape),
                  _const_spec((1, d)), _const_spec((1, d))],
        out_specs=pl.BlockSpec((tm, d), lambda i: (i, 0)),
        scratch_shapes=[pltpu.VMEM((V7X_SUBLANES, C_CONV), _F32)],
        compiler_params=_params(1),
        name="merge_ln",
    )(x, ya, yb, wcg, convw, bgate, wbr, wout, g, b)


def _mem_kv_kernel(mem_ref, wkv_ref, k_ref, v_ref):
    kv = _dot(mem_ref[0].astype(_BF16), wkv_ref[...])
    w = H_X * DH_X
    k_ref[0] = kv[:, :w].astype(_BF16)
    v_ref[0] = kv[:, w:].astype(_BF16)


def _mem_kv(mem, wkv):
    bsz, m, d = mem.shape
    w = H_X * DH_X
    spec = pl.BlockSpec((1, m, w), lambda b: (b, 0, 0))
    return pl.pallas_call(
        _mem_kv_kernel,
        out_shape=[jax.ShapeDtypeStruct((bsz, m, w), _BF16)] * 2,
        grid=(bsz,),
        in_specs=[pl.BlockSpec((1, m, d), lambda b: (b, 0, 0)), _const_spec(wkv.shape)],
        out_specs=[spec, spec],
        compiler_params=_params(1),
        name="mem_kv",
    )(mem, wkv)


def _xattn_ln_kernel(x_ref, k_ref, v_ref, wq_ref, wo_ref, g_ref, b_ref, o_ref, *, alpha):
    x = x_ref[...]
    q = _dot(x.astype(_BF16), wq_ref[...]).astype(_BF16)
    k = k_ref[0]
    v = v_ref[0]
    outs = []
    for h in range(H_X):
        hs = slice(h * DH_X, (h + 1) * DH_X)
        lt = _dot_nt(k[:, hs], q[:, hs]) * (DH_X ** -0.5)
        p = jnp.exp(lt - jnp.max(lt, axis=0, keepdims=True))
        p = p * (1.0 / jnp.sum(p, axis=0, keepdims=True))
        outs.append(_dot_tn(p.astype(_BF16), v[:, hs]))
    o = jnp.concatenate(outs, axis=1).astype(_BF16)
    y = alpha * x + _dot(o, wo_ref[...])
    o_ref[...] = _layer_norm(y, g_ref[...], b_ref[...])


def _xattn_ln(x, k, v, wq, wo, g, b, *, alpha, tm, seq):
    t, d = x.shape
    m, w = k.shape[1], k.shape[2]
    tiles_per_seq = seq // tm
    kv_spec = pl.BlockSpec((1, m, w), lambda i: (i // tiles_per_seq, 0, 0))
    return pl.pallas_call(
        functools.partial(_xattn_ln_kernel, alpha=alpha),
        out_shape=jax.ShapeDtypeStruct((t, d), _F32),
        grid=(t // tm,),
        in_specs=[pl.BlockSpec((tm, d), lambda i: (i, 0)), kv_spec, kv_spec,
                  _const_spec(wq.shape), _const_spec(wo.shape),
                  _const_spec((1, d)), _const_spec((1, d))],
        out_specs=pl.BlockSpec((tm, d), lambda i: (i, 0)),
        compiler_params=_params(1),
        name="xattn_ln",
    )(x, k, v, wq, wo, g, b)


def _t5_bucket(n):
    max_exact = N_BUCKETS // 2
    n = np.maximum(n, 0)
    nf = np.maximum(n, 1).astype(np.float32)
    ratio = np.log(nf / np.float32(max_exact)) / np.float32(math.log(MAX_DISTANCE / max_exact))
    large = max_exact + (ratio * np.float32(N_BUCKETS - max_exact)).astype(np.int32)
    large = np.minimum(large, N_BUCKETS - 1)
    return np.where(n < max_exact, n, large).astype(np.int32)


def _bucket_tiles():
    bq = BLOCK_Q
    s = np.arange(bq, dtype=np.int32)[:, None]
    t = np.arange(bq, dtype=np.int32)[None, :]
    return np.stack([_t5_bucket(d * bq + t - s) for d in range(3)])


def _strict_upper(n):
    r = np.arange(n)
    return (r[None, :] > r[:, None])


def kernel(x, mem, ln_g, ln_b, ffn_w_in, ffn_w_out, w_mix_in, b_gate, kv_norm_g, w_uk, w_uv, conv_w,
           w_branch, w_mix_out, xa_wq, xa_wkv, xa_wo, rel_bias):
    bsz, seq, d = x.shape
    depth = ln_g.shape[0]
    d_ff = ffn_w_out.shape[2]
    t = bsz * seq
    alpha = (2 * depth) ** 0.25
    k_top = min(TOPK_MAX, seq // 4)
    bq = BLOCK_Q
    kc = min(4 * bq, seq)
    tm = min(512, seq)
    tf = 256

    offs = np.cumsum((H_A * DH_A, DC_A, H_IDX * DH_IDX, DH_IDX, H_IDX, W_BRANCH, W_BRANCH, W_BRANCH,
                      C_CONV, C_CONV, C_CONV, d, d, d))
    o_qa, o_ckv, o_iq, o_ik, o_iw, o_qs, o_ks, o_vs, o_cb = (0,) + tuple(int(o) for o in offs[:8])

    bucket_tbl = jnp.asarray(_bucket_tiles())
    tri = jnp.asarray(_strict_upper(kc).T, _BF16)
    up = _strict_upper(bq)
    tri2 = jnp.asarray(np.concatenate([up, up], axis=1), _BF16)

    x = x.reshape(t, d)
    for l in range(depth):
        g = ln_g[l][:, None, :]
        b = ln_b[l][:, None, :]
        bf = lambda a: a.astype(_BF16)

        wa, wb = ffn_w_in[l, 0, :, :d_ff], ffn_w_in[l, 0, :, d_ff:]
        x = _ffn_ln(x, bf(wa), bf(wb), bf(ffn_w_out[l, 0]), g[0], b[0], alpha=alpha, tm=tm, tf=tf)

        wm = w_mix_in[l]
        pad = jnp.zeros((d, V7X_LANES - DH_IDX - H_IDX), wm.dtype)
        w_proj = jnp.concatenate(
            [wm[:, o_qa:o_ckv], wm[:, o_iq:o_ik], wm[:, o_qs:o_ks], wm[:, o_ks:o_vs], wm[:, o_vs:o_cb],
             wm[:, o_ckv:o_iq], wm[:, o_ik:o_qs], pad], axis=1)
        qa, iq, qs, ks, vs, ckv, ckvt, ikw = _mix_proj(x, bf(w_proj), kv_norm_g[l][None, :], tm=tm)
        ikw = ikw.reshape(bsz, seq, V7X_LANES)
        ik = ikw[:, :, :DH_IDX].astype(_BF16)
        iwt = jnp.swapaxes(ikw[:, :, DH_IDX:DH_IDX + H_IDX], 1, 2)
        r3 = lambda a: a.reshape(bsz, seq, a.shape[-1])
        y_a = _dsa(r3(qa), r3(iq), iwt, ik, r3(ckv), ckvt, bf(jnp.swapaxes(w_uk[l], 1, 2)), bf(w_uv[l]),
                   bucket_tbl, rel_bias, tri, k_top=k_top, kc=kc)
        y_b = _stick_breaking(r3(qs), r3(ks), r3(vs), tri2)
        x = _merge_ln(x, y_a.reshape(t, -1), y_b.reshape(t, -1), bf(wm[:, o_cb:]), conv_w[l], b_gate[l],
                      bf(w_branch[l]), bf(w_mix_out[l]), g[1], b[1], alpha=alpha, tm=tm, seq=seq)

        k_mem, v_mem = _mem_kv(mem, bf(xa_wkv[l]))
        x = _xattn_ln(x, k_mem, v_mem, bf(xa_wq[l]), bf(xa_wo[l]), g[2], b[2], alpha=alpha, tm=tm, seq=seq)

        wa, wb = ffn_w_in[l, 1, :, :d_ff], ffn_w_in[l, 1, :, d_ff:]
        x = _ffn_ln(x, bf(wa), bf(wb), bf(ffn_w_out[l, 1]), g[3], b[3], alpha=alpha, tm=tm, tf=tf)
    return x.reshape(bsz, seq, d)
```

```python
import functools
import math

import jax
import jax.numpy as jnp
import numpy as np
from jax import lax
from jax.experimental import pallas as pl
from jax.experimental.pallas import tpu as pltpu

BLOCK_Q = 128
H_A, DH_A, DC_A = 8, 64, 128
DC_AUG = DC_A + 16
H_IDX, DH_IDX = 8, 64
TOPK_MAX = 256
H_SB, DH_SB = 8, 64
C_CONV, CONV_W = 512, 3
W_BRANCH = 512
H_X, DH_X = 4, 128
N_BUCKETS, MAX_DISTANCE = 32, 128
LN_EPS = 1e-5

V7X_LANES = 128
V7X_SUBLANES = 8
V7X_VMEM_LIMIT_BYTES = 56 * 1024 * 1024

_F32 = jnp.float32
_BF16 = jnp.bfloat16
_I32 = jnp.int32

_INT_MIN = -(2 ** 31)
_NEG_INF_KEY = np.int32(np.uint32(0x807FFFFF).view(np.int32))
_MASK_VALUE = -1e30
_LOG2E = math.log2(math.e)
_SB_DEAD_LOG = -160.0
_SB_GROUP = 8


def _dot(a, b):
    return jnp.dot(a, b, preferred_element_type=_F32)


def _dot_nt(a, b):
    return lax.dot_general(a, b, (((1,), (1,)), ((), ())), preferred_element_type=_F32)


def _dot_tn(a, b):
    return lax.dot_general(a, b, (((0,), (0,)), ((), ())), preferred_element_type=_F32)


def _layer_norm(y, g, b):
    mu = jnp.mean(y, axis=-1, keepdims=True)
    yc = y - mu
    var = jnp.mean(yc * yc, axis=-1, keepdims=True)
    return yc * lax.rsqrt(var + LN_EPS) * g + b


def _params(n_grid, **kw):
    return pltpu.CompilerParams(
        dimension_semantics=("arbitrary",) * n_grid,
        vmem_limit_bytes=V7X_VMEM_LIMIT_BYTES, **kw)


def _const_spec(shape):
    n = len(shape)
    return pl.BlockSpec(shape, lambda *_: (0,) * n, pipeline_mode=pl.Buffered(1))


def _ffn_ln_kernel(x_ref, wa_ref, wb_ref, wo_ref, g_ref, b_ref, o_ref, *, alpha, tf):
    x = x_ref[...]
    xb = x.astype(_BF16)
    d_ff = wa_ref.shape[1]
    acc = jnp.zeros(x.shape, _F32)
    for c in range(d_ff // tf):
        sl = slice(c * tf, (c + 1) * tf)
        a = _dot(xb, wa_ref[:, sl])
        b = _dot(xb, wb_ref[:, sl])
        h = (a * jax.nn.sigmoid(a) * b).astype(_BF16)
        acc = acc + _dot(h, wo_ref[sl, :])
    y = alpha * x + 0.5 * acc
    o_ref[...] = _layer_norm(y, g_ref[...], b_ref[...])


def _ffn_ln(x, wa, wb, wo, g, b, *, alpha, tm, tf):
    t, d = x.shape
    f = wa.shape[1]
    return pl.pallas_call(
        functools.partial(_ffn_ln_kernel, alpha=alpha, tf=tf),
        out_shape=jax.ShapeDtypeStruct((t, d), _F32),
        grid=(t // tm,),
        in_specs=[pl.BlockSpec((tm, d), lambda i: (i, 0)),
                  _const_spec((d, f)), _const_spec((d, f)), _const_spec((f, d)),
                  _const_spec((1, d)), _const_spec((1, d))],
        out_specs=pl.BlockSpec((tm, d), lambda i: (i, 0)),
        compiler_params=_params(1),
        name="ffn_ln",
    )(x, wa, wb, wo, g, b)


def _mix_proj_kernel(x_ref, w_ref, kvg_ref, qa_ref, iq_ref, qs_ref, ks_ref, vs_ref, ckv_ref, ckvt_ref,
                     ikw_ref):
    xb = x_ref[...].astype(_BF16)
    wide = W_BRANCH
    for j, ref in enumerate((qa_ref, iq_ref, qs_ref, ks_ref, vs_ref)):
        ref[...] = _dot(xb, w_ref[:, j * wide:(j + 1) * wide]).astype(_BF16)
    off = 5 * wide
    ckv = _dot(xb, w_ref[:, off:off + DC_A])
    ckv = ckv * lax.rsqrt(jnp.mean(ckv * ckv, axis=-1, keepdims=True) + LN_EPS) * kvg_ref[...]
    ckv_ref[...] = ckv.astype(_BF16)
    ones = jnp.ones((DC_AUG - DC_A, ckv.shape[0]), _F32)
    ckvt_ref[...] = jnp.concatenate([ckv.T, ones], axis=0).astype(_BF16)
    ikw_ref[...] = _dot(xb, w_ref[:, off + DC_A:off + DC_A + V7X_LANES])


def _mix_proj(x, w, kvg, *, tm):
    t, d = x.shape
    n = w.shape[1]
    wide_spec = pl.BlockSpec((tm, W_BRANCH), lambda i: (i, 0))
    lane_spec = pl.BlockSpec((tm, V7X_LANES), lambda i: (i, 0))
    return pl.pallas_call(
        _mix_proj_kernel,
        out_shape=[jax.ShapeDtypeStruct((t, W_BRANCH), _BF16)] * 5
        + [jax.ShapeDtypeStruct((t, DC_A), _BF16), jax.ShapeDtypeStruct((DC_AUG, t), _BF16),
           jax.ShapeDtypeStruct((t, V7X_LANES), _F32)],
        grid=(t // tm,),
        in_specs=[pl.BlockSpec((tm, d), lambda i: (i, 0)), _const_spec((d, n)), _const_spec((1, DC_A))],
        out_specs=[wide_spec] * 5 + [lane_spec, pl.BlockSpec((DC_AUG, tm), lambda i: (0, i)), lane_spec],
        compiler_params=_params(1),
        name="mix_proj",
    )(x, w, kvg)


def _bit_transpose32(words):
    a = list(words)
    j, m = 16, 0x0000FFFF
    while j:
        mask = np.int32(np.uint32(m).view(np.int32))
        for k in range(32):
            if k & j == 0:
                t = ((a[k] >> j) ^ a[k + j]) & mask
                a[k] = a[k] ^ (t << j)
                a[k + j] = a[k + j] ^ t
        j >>= 1
        m ^= (m << j) & 0xFFFFFFFF
    return a


def _two_stage_pipeline(n, first, second, carry):
    carry = first(0, 0, carry)

    def pair(k, carry):
        c = 2 * k
        carry = first(c + 1, 1, carry)
        second(c, 0)
        carry = first(c + 2, 0, carry)
        second(c + 1, 1)
        return carry

    n_pairs = (n - 1) // 2
    carry = lax.fori_loop(0, n_pairs, pair, carry)
    c_tail = 2 * n_pairs

    def even_tail():
        first(c_tail + 1, 1, carry)
        second(c_tail, 0)
        second(c_tail + 1, 1)

    def odd_tail():
        second(c_tail, 0)

    lax.cond(n - c_tail == 2, even_tail, odd_tail)


def _dsa_kernel(qa_ref, iq_ref, iwt_ref, ik_ref, ckv_ref, ckvt_ref, wukt_ref, wuv_ref, bucket_ref,
                relb_ref, tri_ref, o_ref, keys_ref, planes_ref, bias_ref, qlat_ref, lt_ref, alpha_ref,
                mnew_ref, acc_ref, m_ref, *, k_top, kc):
    bq = BLOCK_Q
    b_idx = pl.program_id(0)
    i = pl.program_id(1)
    q0 = i * bq
    n_chunks = (q0 + bq + kc - 1) // kc
    sub = kc // bq
    groups = kc // 256
    words_c = kc // 32
    n_words = planes_ref.shape[1]

    @pl.when((b_idx == 0) & (i == 0))
    def _():
        planes_ref[...] = jnp.zeros(planes_ref.shape, _I32)
        bias_ref[...] = jnp.zeros(bias_ref.shape, _F32)

        def fill(bkt, carry):
            hit = bucket_ref[...] == bkt
            for h in range(H_A):
                bias_ref[:, h] = jnp.where(hit, relb_ref[bkt, h] * _LOG2E, bias_ref[:, h])
            return carry

        lax.fori_loop(0, N_BUCKETS, fill, 0)

    qa = qa_ref[0]
    iq = iq_ref[0]
    iq_rows = jnp.concatenate([iq[:, h * DH_IDX:(h + 1) * DH_IDX] for h in range(H_IDX)], axis=0)
    w_idx = iwt_ref[0] * ((H_IDX ** -0.5) * (DH_IDX ** -0.5))
    t_idx = q0 + lax.broadcasted_iota(_I32, (1, bq), 1)

    def stage_dots(c, slot, carry):
        s0 = pl.multiple_of(c * kc, kc)
        lt_ref[slot] = _dot_nt(ik_ref[0, pl.ds(s0, kc), :], iq_rows)
        return carry

    def stage_keys(c, slot):
        grp = kc // groups
        for g in range(groups):
            s0 = pl.multiple_of(c * kc + g * grp, grp)
            rows = slice(g * grp, (g + 1) * grp)
            sc = jnp.zeros((grp, bq), _F32)
            for h in range(H_IDX):
                sc = sc + jnp.maximum(lt_ref[slot, rows, h * bq:(h + 1) * bq], 0.0) * w_idx[h:h + 1, :]
            s_idx = s0 + lax.broadcasted_iota(_I32, (grp, bq), 0)
            sc = jnp.where(s_idx <= t_idx, sc, -jnp.inf)
            bits = pltpu.bitcast(sc, _I32)
            keys = bits ^ ((bits >> 31) & 0x7FFFFFFF)
            keys_ref[pl.ds(s0, grp), :] = keys
            u = (keys ^ _INT_MIN).reshape(32, V7X_SUBLANES, bq)
            planes = _bit_transpose32([u[j] for j in range(32)])
            w0 = pl.multiple_of((c * groups + g) * V7X_SUBLANES, V7X_SUBLANES)
            for b in range(32):
                planes_ref[b, pl.ds(w0, V7X_SUBLANES), :] = planes[b]

    _two_stage_pipeline(n_chunks, stage_dots, stage_keys, 0)

    word_row = lax.broadcasted_iota(_I32, (n_words, bq), 0)
    alive0 = jnp.where(word_row < n_chunks * words_c, -1, 0)

    def bit_step(it, state):
        alive, need, thr_u = state
        b = 31 - it
        ones = alive & planes_ref[b]
        cnt = lax.population_count(ones)
        cnt = jnp.sum(cnt.reshape(n_words // V7X_SUBLANES, V7X_SUBLANES, bq), axis=0)
        cnt = jnp.sum(cnt, axis=0, keepdims=True)
        take = cnt >= need
        alive = jnp.where(take, ones, alive ^ ones)
        need = jnp.where(take, need, need - cnt)
        thr_u = thr_u | jnp.where(take, lax.shift_left(jnp.int32(1), b), 0)
        return alive, need, thr_u

    def sublane_fold(a, op):
        return op(a.reshape(a.shape[0] // V7X_SUBLANES, V7X_SUBLANES, a.shape[1]), axis=0)

    alive, need, thr_u = lax.fori_loop(
        0, 32, bit_step, (alive0, jnp.full((1, bq), k_top, _I32), jnp.zeros((1, bq), _I32)))
    thr = thr_u ^ _INT_MIN
    n_ties = jnp.where(thr > _NEG_INF_KEY, need, 0)
    thr_gt = jnp.maximum(thr, _NEG_INF_KEY)
    n_equal = jnp.sum(sublane_fold(lax.population_count(alive), jnp.sum), axis=0, keepdims=True)
    ranked = jnp.max(jnp.where(thr > _NEG_INF_KEY, n_equal - need, 0)) > 0

    q_rows = _dot_nt(wukt_ref[...], qa) * ((DH_A ** -0.5) * _LOG2E)
    for h in range(H_A):
        qlat_ref[:, h * bq:(h + 1) * bq] = q_rows[h * DC_A:(h + 1) * DC_A, :].astype(_BF16)

    m_ref[...] = jnp.full(m_ref.shape, -jnp.inf, _F32)
    acc_ref[...] = jnp.zeros(acc_ref.shape, _F32)

    def stage_logits(c, slot, ties_before, use_rank):
        s0 = pl.multiple_of(c * kc, kc)
        keys = keys_ref[pl.ds(s0, kc), :]
        if use_rank:
            eq = keys == thr
            eq_f = jnp.where(eq, 1.0, 0.0)
            rank = ties_before + _dot(tri_ref[...], eq_f.astype(_BF16))
            sel = (keys > thr_gt) | (eq & (rank < n_ties.astype(_F32)))
            ties_before = ties_before + jnp.sum(sublane_fold(eq_f, jnp.sum), axis=0, keepdims=True)
        else:
            sel = keys >= jnp.maximum(thr, _NEG_INF_KEY + 1)
        mask_add = jnp.where(sel, 0.0, _MASK_VALUE)

        ckv_c = ckv_ref[0, pl.ds(s0, kc), :]
        d_blk = [jnp.clip(i - (c * sub + j), 0, 2) for j in range(sub)]
        for hp in range(H_A // 2):
            logits = _dot(ckv_c, qlat_ref[:, hp * 2 * bq:(hp + 1) * 2 * bq])
            for hh in range(2):
                h = 2 * hp + hh
                hs = slice(h * bq, (h + 1) * bq)
                m_fold = None
                for j in range(sub):
                    rs = slice(j * bq, (j + 1) * bq)
                    lt = logits[rs, hh * bq:(hh + 1) * bq] + bias_ref[d_blk[j], h] + mask_add[rs, :]
                    lt_ref[slot, rs, hs] = lt
                    lt_fold = sublane_fold(lt, jnp.max)
                    m_fold = lt_fold if m_fold is None else jnp.maximum(m_fold, lt_fold)
                m_old = m_ref[:, hs]
                m_new = jnp.maximum(m_old, jnp.max(m_fold, axis=0, keepdims=True))
                alpha_ref[slot, :, hs] = jnp.exp2(m_old - m_new)
                mnew_ref[slot, :, hs] = m_new
                m_ref[:, hs] = m_new
        return ties_before

    def stage_accumulate(c, slot):
        s0 = pl.multiple_of(c * kc, kc)
        ckvt_c = ckvt_ref[:, pl.ds(s0, kc)]
        for hp in range(H_A // 2):
            cs = slice(hp * 2 * bq, (hp + 1) * 2 * bq)
            p = jnp.exp2(lt_ref[slot, :, cs] - mnew_ref[slot, :, cs])
            acc_ref[:, cs] = acc_ref[:, cs] * alpha_ref[slot, :, cs] + _dot(ckvt_c, p.astype(_BF16))

    def attend(use_rank):
        first = functools.partial(stage_logits, use_rank=use_rank)
        _two_stage_pipeline(n_chunks, first, stage_accumulate, jnp.zeros((1, bq), _F32))

    lax.cond(ranked, lambda: attend(True), lambda: attend(False))

    inv_l = 1.0 / acc_ref[DC_A:DC_A + 1, :]
    o_lat_t = (acc_ref[0:DC_A, :] * inv_l).astype(_BF16)
    o_rows = jnp.concatenate([o_lat_t[:, h * bq:(h + 1) * bq] for h in range(H_A)], axis=0)
    o_ref[0] = _dot_tn(o_rows, wuv_ref[...]).astype(_BF16)


def _dsa(qa, iq, iwt, ik, ckv, ckvt, wukt, wuv, bucket_tbl, rel_bias, tri, *, k_top, kc):
    bsz, seq, _ = qa.shape
    bq = BLOCK_Q
    q_spec = pl.BlockSpec((1, bq, H_A * DH_A), lambda b, i: (b, i, 0))
    return pl.pallas_call(
        functools.partial(_dsa_kernel, k_top=k_top, kc=kc),
        out_shape=jax.ShapeDtypeStruct((bsz, seq, H_A * DH_A), _BF16),
        grid=(bsz, seq // bq),
        in_specs=[q_spec, q_spec,
                  pl.BlockSpec((1, H_IDX, bq), lambda b, i: (b, 0, i)),
                  pl.BlockSpec((1, seq, DH_IDX), lambda b, i: (b, 0, 0)),
                  pl.BlockSpec((1, seq, DC_A), lambda b, i: (b, 0, 0)),
                  pl.BlockSpec((DC_AUG, seq), lambda b, i: (0, b)),
                  _const_spec(wukt.shape), _const_spec(wuv.shape), _const_spec(bucket_tbl.shape),
                  pl.BlockSpec(memory_space=pltpu.SMEM),
                  _const_spec(tri.shape)],
        out_specs=q_spec,
        scratch_shapes=[pltpu.VMEM((seq, bq), _I32),
                        pltpu.VMEM((32, seq // 32, bq), _I32),
                        pltpu.VMEM((3, H_A, bq, bq), _F32),
                        pltpu.VMEM((DC_A, H_A * bq), _BF16),
                        pltpu.VMEM((2, kc, H_A * bq), _F32),
                        pltpu.VMEM((2, 1, H_A * bq), _F32),
                        pltpu.VMEM((2, 1, H_A * bq), _F32),
                        pltpu.VMEM((DC_AUG, H_A * bq), _F32),
                        pltpu.VMEM((1, H_A * bq), _F32)],
        compiler_params=_params(2),
        name="dsa",
    )(qa, iq, iwt, ik, ckv, ckvt, wukt, wuv, bucket_tbl, rel_bias, tri)


def _sb_kernel(q_ref, k_ref, v_ref, tri2_ref, o_ref, acc_ref, run_ref):
    bq = BLOCK_Q
    i = pl.program_id(1)
    q = q_ref[0]
    scale = DH_SB ** -0.5
    run_ref[...] = jnp.zeros(run_ref.shape, _F32)
    acc_ref[...] = jnp.zeros(acc_ref.shape, _F32)
    row = lax.broadcasted_iota(_I32, (bq, bq), 0)
    col = lax.broadcasted_iota(_I32, (bq, bq), 1)

    def cond(state):
        j, run_max = state
        return (j >= 0) & (run_max > _SB_DEAD_LOG)

    def body(state):
        j, _ = state
        s0 = pl.multiple_of(j * bq, bq)
        kb = k_ref[0, pl.ds(s0, bq), :]
        vb = v_ref[0, pl.ds(s0, bq), :]
        strict = row + (j - i) * bq < col
        run_max = None
        for grp in range(H_SB // _SB_GROUP):
            heads = range(grp * _SB_GROUP, (grp + 1) * _SB_GROUP)
            ys, log_nots = [], []
            for h in heads:
                hs = slice(h * DH_SB, (h + 1) * DH_SB)
                y = _dot_nt(kb[:, hs], q[:, hs]) * (-scale * _LOG2E)
                y = jnp.where(strict, y, -_MASK_VALUE)
                ys.append(y)
                log_nots.append(jnp.minimum(y, 0.0) - jnp.log2(1.0 + jnp.exp2(-jnp.abs(y))))
            log_not = jnp.concatenate(log_nots, axis=1)
            hi = log_not.astype(_BF16)
            lo = (log_not - hi.astype(_F32)).astype(_BF16)
            gs = slice(grp * _SB_GROUP * bq, (grp + 1) * _SB_GROUP * bq)
            later = _dot(tri2_ref[...], jnp.concatenate([hi, lo], axis=0)) + run_ref[:, gs]
            outs = []
            for n, h in enumerate(heads):
                a = jnp.exp2(log_nots[n] - ys[n] + later[:, n * bq:(n + 1) * bq])
                outs.append(_dot_tn(vb[:, h * DH_SB:(h + 1) * DH_SB], a.astype(_BF16)))
            ds = slice(grp * _SB_GROUP * DH_SB, (grp + 1) * _SB_GROUP * DH_SB)
            acc_ref[ds, :] += jnp.concatenate(outs, axis=0)
            fold = jnp.sum(log_not.reshape(bq // V7X_SUBLANES, V7X_SUBLANES, log_not.shape[1]), axis=0)
            run = run_ref[:, gs] + jnp.sum(fold, axis=0, keepdims=True)
            run_ref[:, gs] = run
            run_max = jnp.max(run) if run_max is None else jnp.maximum(run_max, jnp.max(run))
        return j - 1, run_max

    lax.while_loop(cond, body, (i, jnp.float32(0.0)))
    o_ref[0] = acc_ref[...].T.astype(_BF16)


def _stick_breaking(q, k, v, tri2):
    bsz, seq, width = q.shape
    bq = BLOCK_Q
    return pl.pallas_call(
        _sb_kernel,
        out_shape=jax.ShapeDtypeStruct((bsz, seq, width), _BF16),
        grid=(bsz, seq // bq),
        in_specs=[pl.BlockSpec((1, bq, width), lambda b, i: (b, i, 0)),
                  pl.BlockSpec((1, seq, width), lambda b, i: (b, 0, 0)),
                  pl.BlockSpec((1, seq, width), lambda b, i: (b, 0, 0)),
                  _const_spec(tri2.shape)],
        out_specs=pl.BlockSpec((1, bq, width), lambda b, i: (b, i, 0)),
        scratch_shapes=[pltpu.VMEM((width, bq), _F32), pltpu.VMEM((1, H_SB * bq), _F32)],
        compiler_params=_params(2),
        name="stick_breaking",
    )(q, k, v, tri2)


def _merge_ln_kernel(x_ref, ya_ref, yb_ref, wcg_ref, convw_ref, bgate_ref, wbr_ref, wout_ref,
                     g_ref, b_ref, o_ref, halo_ref, *, alpha, tiles_per_seq):
    i = pl.program_id(0)
    x = x_ref[...]
    xb = x.astype(_BF16)
    tm = x.shape[0]
    c = C_CONV

    @pl.when(i % tiles_per_seq == 0)
    def _():
        halo_ref[...] = jnp.zeros(halo_ref.shape, _F32)

    cb = _dot(xb, wcg_ref[:, 0:c])
    z = _dot(xb, wcg_ref[:, c:2 * c]) * _dot(xb, wcg_ref[:, 2 * c:3 * c])
    halo = halo_ref[...]
    row = lax.broadcasted_iota(_I32, (tm, c), 0)
    last = V7X_SUBLANES - 1
    z1 = jnp.where(row == 0, halo[last:last + 1, :], pltpu.roll(z, 1, 0))
    z2 = jnp.where(row == 0, halo[last - 1:last, :],
                   jnp.where(row == 1, halo[last:last + 1, :], pltpu.roll(z, 2, 0)))
    halo_ref[...] = z[tm - V7X_SUBLANES:, :]
    y_c = cb * (convw_ref[0:1, :] * z2 + convw_ref[1:2, :] * z1 + convw_ref[2:3, :] * z)

    d = x.shape[1]
    merged = jnp.zeros((tm, d), _F32)
    branches = (ya_ref[...], yb_ref[...], y_c.astype(_BF16))
    for n, y in enumerate(branches):
        gate = _dot(xb, wcg_ref[:, 3 * c + n * d:3 * c + (n + 1) * d]) + bgate_ref[n:n + 1, :]
        merged = merged + jax.nn.sigmoid(gate) * _dot(y, wbr_ref[n])
    y = alpha * x + _dot(merged.astype(_BF16), wout_ref[...])
    o_ref[...] = _layer_norm(y, g_ref[...], b_ref[...])


def _merge_ln(x, ya, yb, wcg, convw, bgate, wbr, wout, g, b, *, alpha, tm, seq):
    t, d = x.shape
    return pl.pallas_call(
        functools.partial(_merge_ln_kernel, alpha=alpha, tiles_per_seq=seq // tm),
        out_shape=jax.ShapeDtypeStruct((t, d), _F32),
        grid=(t // tm,),
        in_specs=[pl.BlockSpec((tm, d), lambda i: (i, 0)),
                  pl.BlockSpec((tm, W_BRANCH), lambda i: (i, 0)),
                  pl.BlockSpec((tm, W_BRANCH), lambda i: (i, 0)),
                  _const_spec(wcg.shape), _const_spec(convw.shape), _const_spec(bgate.shape),
                  _const_spec(wbr.shape), _const_spec(wout.shape),
                  _const_spec((1, d)), _const_spec((1, d))],
        out_specs=pl.BlockSpec((tm, d), lambda i: (i, 0)),
        scratch_shapes=[pltpu.VMEM((V7X_SUBLANES, C_CONV), _F32)],
        compiler_params=_params(1),
        name="merge_ln",
    )(x, ya, yb, wcg, convw, bgate, wbr, wout, g, b)


def _mem_kv_kernel(mem_ref, wkv_ref, k_ref, v_ref):
    kv = _dot(mem_ref[0].astype(_BF16), wkv_ref[...])
    w = H_X * DH_X
    k_ref[0] = kv[:, :w].astype(_BF16)
    v_ref[0] = kv[:, w:].astype(_BF16)


def _mem_kv(mem, wkv):
    bsz, m, d = mem.shape
    w = H_X * DH_X
    spec = pl.BlockSpec((1, m, w), lambda b: (b, 0, 0))
    return pl.pallas_call(
        _mem_kv_kernel,
        out_shape=[jax.ShapeDtypeStruct((bsz, m, w), _BF16)] * 2,
        grid=(bsz,),
        in_specs=[pl.BlockSpec((1, m, d), lambda b: (b, 0, 0)), _const_spec(wkv.shape)],
        out_specs=[spec, spec],
        compiler_params=_params(1),
        name="mem_kv",
    )(mem, wkv)


def _xattn_ln_kernel(x_ref, k_ref, v_ref, wq_ref, wo_ref, g_ref, b_ref, o_ref, *, alpha):
    x = x_ref[...]
    q = _dot(x.astype(_BF16), wq_ref[...]).astype(_BF16)
    k = k_ref[0]
    v = v_ref[0]
    outs = []
    for h in range(H_X):
        hs = slice(h * DH_X, (h + 1) * DH_X)
        lt = _dot_nt(k[:, hs], q[:, hs]) * (DH_X ** -0.5)
        p = jnp.exp(lt - jnp.max(lt, axis=0, keepdims=True))
        p = p * (1.0 / jnp.sum(p, axis=0, keepdims=True))
        outs.append(_dot_tn(p.astype(_BF16), v[:, hs]))
    o = jnp.concatenate(outs, axis=1).astype(_BF16)
    y = alpha * x + _dot(o, wo_ref[...])
    o_ref[...] = _layer_norm(y, g_ref[...], b_ref[...])


def _xattn_ln(x, k, v, wq, wo, g, b, *, alpha, tm, seq):
    t, d = x.shape
    m, w = k.shape[1], k.shape[2]
    tiles_per_seq = seq // tm
    kv_spec = pl.BlockSpec((1, m, w), lambda i: (i // tiles_per_seq, 0, 0))
    return pl.pallas_call(
        functools.partial(_xattn_ln_kernel, alpha=alpha),
        out_shape=jax.ShapeDtypeStruct((t, d), _F32),
        grid=(t // tm,),
        in_specs=[pl.BlockSpec((tm, d), lambda i: (i, 0)), kv_spec, kv_spec,
                  _const_spec(wq.shape), _const_spec(wo.shape),
                  _const_spec((1, d)), _const_spec((1, d))],
        out_specs=pl.BlockSpec((tm, d), lambda i: (i, 0)),
        compiler_params=_params(1),
        name="xattn_ln",
    )(x, k, v, wq, wo, g, b)


def _t5_bucket(n):
    max_exact = N_BUCKETS // 2
    n = np.maximum(n, 0)
    nf = np.maximum(n, 1).astype(np.float32)
    ratio = np.log(nf / np.float32(max_exact)) / np.float32(math.log(MAX_DISTANCE / max_exact))
    large = max_exact + (ratio * np.float32(N_BUCKETS - max_exact)).astype(np.int32)
    large = np.minimum(large, N_BUCKETS - 1)
    return np.where(n < max_exact, n, large).astype(np.int32)


def _bucket_tiles():
    bq = BLOCK_Q
    s = np.arange(bq, dtype=np.int32)[:, None]
    t = np.arange(bq, dtype=np.int32)[None, :]
    return np.stack([_t5_bucket(d * bq + t - s) for d in range(3)])


def _strict_upper(n):
    r = np.arange(n)
    return (r[None, :] > r[:, None])


def kernel(x, mem, ln_g, ln_b, ffn_w_in, ffn_w_out, w_mix_in, b_gate, kv_norm_g, w_uk, w_uv, conv_w,
           w_branch, w_mix_out, xa_wq, xa_wkv, xa_wo, rel_bias):
    bsz, seq, d = x.shape
    depth = ln_g.shape[0]
    d_ff = ffn_w_out.shape[2]
    t = bsz * seq
    alpha = (2 * depth) ** 0.25
    k_top = min(TOPK_MAX, seq // 4)
    bq = BLOCK_Q
    kc = min(4 * bq, seq)
    tm = min(512, seq)
    tf = 256

    offs = np.cumsum((H_A * DH_A, DC_A, H_IDX * DH_IDX, DH_IDX, H_IDX, W_BRANCH, W_BRANCH, W_BRANCH,
                      C_CONV, C_CONV, C_CONV, d, d, d))
    o_qa, o_ckv, o_iq, o_ik, o_iw, o_qs, o_ks, o_vs, o_cb = (0,) + tuple(int(o) for o in offs[:8])

    bucket_tbl = jnp.asarray(_bucket_tiles())
    tri = jnp.asarray(_strict_upper(kc).T, _BF16)
    up = _strict_upper(bq)
    tri2 = jnp.asarray(np.concatenate([up, up], axis=1), _BF16)

    x = x.reshape(t, d)
    for l in range(depth):
        g = ln_g[l][:, None, :]
        b = ln_b[l][:, None, :]
        bf = lambda a: a.astype(_BF16)

        wa, wb = ffn_w_in[l, 0, :, :d_ff], ffn_w_in[l, 0, :, d_ff:]
        x = _ffn_ln(x, bf(wa), bf(wb), bf(ffn_w_out[l, 0]), g[0], b[0], alpha=alpha, tm=tm, tf=tf)

        wm = w_mix_in[l]
        pad = jnp.zeros((d, V7X_LANES - DH_IDX - H_IDX), wm.dtype)
        w_proj = jnp.concatenate(
            [wm[:, o_qa:o_ckv], wm[:, o_iq:o_ik], wm[:, o_qs:o_ks], wm[:, o_ks:o_vs], wm[:, o_vs:o_cb],
             wm[:, o_ckv:o_iq], wm[:, o_ik:o_qs], pad], axis=1)
        qa, iq, qs, ks, vs, ckv, ckvt, ikw = _mix_proj(x, bf(w_proj), kv_norm_g[l][None, :], tm=tm)
        ikw = ikw.reshape(bsz, seq, V7X_LANES)
        ik = ikw[:, :, :DH_IDX].astype(_BF16)
        iwt = jnp.swapaxes(ikw[:, :, DH_IDX:DH_IDX + H_IDX], 1, 2)
        r3 = lambda a: a.reshape(bsz, seq, a.shape[-1])
        wuk_bd = jax.scipy.linalg.block_diag(*[w_uk[l, h].T for h in range(H_A)])
        wuv_bd = jax.scipy.linalg.block_diag(*[w_uv[l, h] for h in range(H_A)])
        y_a = _dsa(r3(qa), r3(iq), iwt, ik, r3(ckv), ckvt, bf(wuk_bd), bf(wuv_bd),
                   bucket_tbl, rel_bias, tri, k_top=k_top, kc=kc)
        y_b = _stick_breaking(r3(qs), r3(ks), r3(vs), tri2)
        x = _merge_ln(x, y_a.reshape(t, -1), y_b.reshape(t, -1), bf(wm[:, o_cb:]), conv_w[l], b_gate[l],
                      bf(w_branch[l]), bf(w_mix_out[l]), g[1], b[1], alpha=alpha, tm=tm, seq=seq)

        k_mem, v_mem = _mem_kv(mem, bf(xa_wkv[l]))
        x = _xattn_ln(x, k_mem, v_mem, bf(xa_wq[l]), bf(xa_wo[l]), g[2], b[2], alpha=alpha, tm=tm, seq=seq)

        wa, wb = ffn_w_in[l, 1, :, :d_ff], ffn_w_in[l, 1, :, d_ff:]
        x = _ffn_ln(x, bf(wa), bf(wb), bf(ffn_w_out[l, 1]), g[3], b[3], alpha=alpha, tm=tm, tf=tf)
    return x.reshape(bsz, seq, d)
```

```python
import functools
import math

import jax
import jax.numpy as jnp
import numpy as np
from jax import lax
from jax.experimental import pallas as pl
from jax.experimental.pallas import tpu as pltpu

BLOCK_Q = 128
H_A, DH_A, DC_A = 8, 64, 128
DC_AUG = DC_A + 16
H_IDX, DH_IDX = 8, 64
TOPK_MAX = 256
H_SB, DH_SB = 8, 64
C_CONV, CONV_W = 512, 3
W_BRANCH = 512
H_X, DH_X = 4, 128
N_BUCKETS, MAX_DISTANCE = 32, 128
LN_EPS = 1e-5

V7X_LANES = 128
V7X_SUBLANES = 8
V7X_VMEM_LIMIT_BYTES = 56 * 1024 * 1024

_F32 = jnp.float32
_BF16 = jnp.bfloat16
_I32 = jnp.int32

_INT_MIN = -(2 ** 31)
_NEG_INF_KEY = np.int32(np.uint32(0x807FFFFF).view(np.int32))
_MASK_VALUE = -1e30
_LOG2E = math.log2(math.e)
_SB_DEAD_LOG = -160.0


def _dot(a, b):
    return jnp.dot(a, b, preferred_element_type=_F32)


def _dot_nt(a, b):
    return lax.dot_general(a, b, (((1,), (1,)), ((), ())), preferred_element_type=_F32)


def _dot_tn(a, b):
    return lax.dot_general(a, b, (((0,), (0,)), ((), ())), preferred_element_type=_F32)


def _layer_norm(y, g, b):
    mu = jnp.mean(y, axis=-1, keepdims=True)
    yc = y - mu
    var = jnp.mean(yc * yc, axis=-1, keepdims=True)
    return yc * lax.rsqrt(var + LN_EPS) * g + b


def _params(n_grid, **kw):
    return pltpu.CompilerParams(
        dimension_semantics=("arbitrary",) * n_grid,
        vmem_limit_bytes=V7X_VMEM_LIMIT_BYTES, **kw)


def _const_spec(shape):
    n = len(shape)
    return pl.BlockSpec(shape, lambda *_: (0,) * n, pipeline_mode=pl.Buffered(1))


def _ffn_ln_kernel(x_ref, wi_ref, wo_ref, g_ref, b_ref, o_ref, *, alpha, tf):
    x = x_ref[...]
    xb = x.astype(_BF16)
    d_ff = wo_ref.shape[0]
    acc = jnp.zeros(x.shape, _F32)
    for c in range(d_ff // tf):
        a = _dot(xb, wi_ref[:, c * tf:(c + 1) * tf])
        b = _dot(xb, wi_ref[:, d_ff + c * tf:d_ff + (c + 1) * tf])
        h = (a * jax.nn.sigmoid(a) * b).astype(_BF16)
        acc = acc + _dot(h, wo_ref[c * tf:(c + 1) * tf, :])
    y = alpha * x + 0.5 * acc
    o_ref[...] = _layer_norm(y, g_ref[...], b_ref[...])


def _ffn_ln(x, wi, wo, g, b, *, alpha, tm, tf):
    t, d = x.shape
    f = wo.shape[0]
    return pl.pallas_call(
        functools.partial(_ffn_ln_kernel, alpha=alpha, tf=tf),
        out_shape=jax.ShapeDtypeStruct((t, d), _F32),
        grid=(t // tm,),
        in_specs=[pl.BlockSpec((tm, d), lambda i: (i, 0)),
                  _const_spec((d, 2 * f)), _const_spec((f, d)),
                  _const_spec((1, d)), _const_spec((1, d))],
        out_specs=pl.BlockSpec((tm, d), lambda i: (i, 0)),
        compiler_params=_params(1),
        name="ffn_ln",
    )(x, wi, wo, g, b)


def _mix_proj_kernel(x_ref, w_ref, kvg_ref, qa_ref, iq_ref, qs_ref, ks_ref, vs_ref, ckv_ref, ckvt_ref,
                     ikw_ref):
    xb = x_ref[...].astype(_BF16)
    wide = W_BRANCH
    for j, ref in enumerate((qa_ref, iq_ref, qs_ref, ks_ref, vs_ref)):
        ref[...] = _dot(xb, w_ref[:, j * wide:(j + 1) * wide]).astype(_BF16)
    off = 5 * wide
    ckv = _dot(xb, w_ref[:, off:off + DC_A])
    ckv = ckv * lax.rsqrt(jnp.mean(ckv * ckv, axis=-1, keepdims=True) + LN_EPS) * kvg_ref[...]
    ckv_ref[...] = ckv.astype(_BF16)
    ones = jnp.ones((DC_AUG - DC_A, ckv.shape[0]), _F32)
    ckvt_ref[...] = jnp.concatenate([ckv.T, ones], axis=0).astype(_BF16)
    ikw_ref[...] = _dot(xb, w_ref[:, off + DC_A:off + DC_A + V7X_LANES])


def _mix_proj(x, w, kvg, *, tm):
    t, d = x.shape
    n = w.shape[1]
    wide_spec = pl.BlockSpec((tm, W_BRANCH), lambda i: (i, 0))
    lane_spec = pl.BlockSpec((tm, V7X_LANES), lambda i: (i, 0))
    return pl.pallas_call(
        _mix_proj_kernel,
        out_shape=[jax.ShapeDtypeStruct((t, W_BRANCH), _BF16)] * 5
        + [jax.ShapeDtypeStruct((t, DC_A), _BF16), jax.ShapeDtypeStruct((DC_AUG, t), _BF16),
           jax.ShapeDtypeStruct((t, V7X_LANES), _F32)],
        grid=(t // tm,),
        in_specs=[pl.BlockSpec((tm, d), lambda i: (i, 0)), _const_spec((d, n)), _const_spec((1, DC_A))],
        out_specs=[wide_spec] * 5 + [lane_spec, pl.BlockSpec((DC_AUG, tm), lambda i: (0, i)), lane_spec],
        compiler_params=_params(1),
        name="mix_proj",
    )(x, w, kvg)


def _bit_transpose32(words):
    a = list(words)
    j, m = 16, 0x0000FFFF
    while j:
        mask = np.int32(np.uint32(m).view(np.int32))
        for k in range(32):
            if k & j == 0:
                t = ((a[k] >> j) ^ a[k + j]) & mask
                a[k] = a[k] ^ (t << j)
                a[k + j] = a[k + j] ^ t
        j >>= 1
        m ^= (m << j) & 0xFFFFFFFF
    return a


def _two_stage_pipeline(n, first, second, carry):
    carry = first(0, 0, carry)

    def pair(k, carry):
        c = 2 * k
        carry = first(c + 1, 1, carry)
        second(c, 0)
        carry = first(c + 2, 0, carry)
        second(c + 1, 1)
        return carry

    n_pairs = (n - 1) // 2
    carry = lax.fori_loop(0, n_pairs, pair, carry)
    c_tail = 2 * n_pairs

    def even_tail():
        first(c_tail + 1, 1, carry)
        second(c_tail, 0)
        second(c_tail + 1, 1)

    def odd_tail():
        second(c_tail, 0)

    lax.cond(n - c_tail == 2, even_tail, odd_tail)


def _dsa_kernel(qa_ref, iq_ref, iwt_ref, ik_ref, ckv_ref, ckvt_ref, wukt_ref, wuv_ref, bucket_ref,
                relb_ref, tri_ref, o_ref, keys_ref, planes_ref, bias_ref, qlat_ref, lt_ref, alpha_ref,
                mnew_ref, acc_ref, m_ref, *, k_top, kc):
    bq = BLOCK_Q
    b_idx = pl.program_id(0)
    i = pl.program_id(1)
    q0 = i * bq
    n_chunks = (q0 + bq + kc - 1) // kc
    sub = kc // bq
    groups = kc // 256
    words_c = kc // 32
    n_words = planes_ref.shape[1]

    @pl.when((b_idx == 0) & (i == 0))
    def _():
        planes_ref[...] = jnp.zeros(planes_ref.shape, _I32)
        bias_ref[...] = jnp.zeros(bias_ref.shape, _F32)

        def fill(bkt, carry):
            hit = bucket_ref[...] == bkt
            for h in range(H_A):
                bias_ref[:, h] = jnp.where(hit, relb_ref[bkt, h] * _LOG2E, bias_ref[:, h])
            return carry

        lax.fori_loop(0, N_BUCKETS, fill, 0)

    qa = qa_ref[0]
    iq = iq_ref[0]
    iq_rows = jnp.concatenate([iq[:, h * DH_IDX:(h + 1) * DH_IDX] for h in range(H_IDX)], axis=0)
    w_idx = iwt_ref[0] * ((H_IDX ** -0.5) * (DH_IDX ** -0.5))
    t_idx = q0 + lax.broadcasted_iota(_I32, (1, bq), 1)

    def stage_dots(c, slot, carry):
        s0 = pl.multiple_of(c * kc, kc)
        lt_ref[slot] = _dot_nt(ik_ref[0, pl.ds(s0, kc), :], iq_rows)
        return carry

    def stage_keys(c, slot):
        grp = kc // groups
        for g in range(groups):
            s0 = pl.multiple_of(c * kc + g * grp, grp)
            rows = slice(g * grp, (g + 1) * grp)
            sc = jnp.zeros((grp, bq), _F32)
            for h in range(H_IDX):
                sc = sc + jnp.maximum(lt_ref[slot, rows, h * bq:(h + 1) * bq], 0.0) * w_idx[h:h + 1, :]
            s_idx = s0 + lax.broadcasted_iota(_I32, (grp, bq), 0)
            sc = jnp.where(s_idx <= t_idx, sc, -jnp.inf)
            bits = pltpu.bitcast(sc, _I32)
            keys = bits ^ ((bits >> 31) & 0x7FFFFFFF)
            keys_ref[pl.ds(s0, grp), :] = keys
            u = (keys ^ _INT_MIN).reshape(32, V7X_SUBLANES, bq)
            planes = _bit_transpose32([u[j] for j in range(32)])
            w0 = pl.multiple_of((c * groups + g) * V7X_SUBLANES, V7X_SUBLANES)
            for b in range(32):
                planes_ref[b, pl.ds(w0, V7X_SUBLANES), :] = planes[b]

    _two_stage_pipeline(n_chunks, stage_dots, stage_keys, 0)

    word_row = lax.broadcasted_iota(_I32, (n_words, bq), 0)
    alive0 = jnp.where(word_row < n_chunks * words_c, -1, 0)

    def bit_step(it, state):
        alive, need, thr_u = state
        b = 31 - it
        ones = alive & planes_ref[b]
        cnt = lax.population_count(ones)
        cnt = jnp.sum(cnt.reshape(n_words // V7X_SUBLANES, V7X_SUBLANES, bq), axis=0)
        cnt = jnp.sum(cnt, axis=0, keepdims=True)
        take = cnt >= need
        alive = jnp.where(take, ones, alive ^ ones)
        need = jnp.where(take, need, need - cnt)
        thr_u = thr_u | jnp.where(take, lax.shift_left(jnp.int32(1), b), 0)
        return alive, need, thr_u

    def sublane_fold(a, op):
        return op(a.reshape(a.shape[0] // V7X_SUBLANES, V7X_SUBLANES, a.shape[1]), axis=0)

    alive, need, thr_u = lax.fori_loop(
        0, 32, bit_step, (alive0, jnp.full((1, bq), k_top, _I32), jnp.zeros((1, bq), _I32)))
    thr = thr_u ^ _INT_MIN
    n_ties = jnp.where(thr > _NEG_INF_KEY, need, 0)
    thr_gt = jnp.maximum(thr, _NEG_INF_KEY)
    n_equal = jnp.sum(sublane_fold(lax.population_count(alive), jnp.sum), axis=0, keepdims=True)
    ranked = jnp.max(jnp.where(thr > _NEG_INF_KEY, n_equal - need, 0)) > 0

    q_rows = _dot_nt(wukt_ref[...], qa) * ((DH_A ** -0.5) * _LOG2E)
    for h in range(H_A):
        qlat_ref[:, h * bq:(h + 1) * bq] = q_rows[h * DC_A:(h + 1) * DC_A, :].astype(_BF16)

    m_ref[...] = jnp.full(m_ref.shape, -jnp.inf, _F32)
    acc_ref[...] = jnp.zeros(acc_ref.shape, _F32)

    def stage_logits(c, slot, ties_before, use_rank):
        s0 = pl.multiple_of(c * kc, kc)
        keys = keys_ref[pl.ds(s0, kc), :]
        if use_rank:
            eq = keys == thr
            eq_f = jnp.where(eq, 1.0, 0.0)
            rank = ties_before + _dot(tri_ref[...], eq_f.astype(_BF16))
            sel = (keys > thr_gt) | (eq & (rank < n_ties.astype(_F32)))
            ties_before = ties_before + jnp.sum(sublane_fold(eq_f, jnp.sum), axis=0, keepdims=True)
        else:
            sel = keys >= jnp.maximum(thr, _NEG_INF_KEY + 1)
        mask_add = jnp.where(sel, 0.0, _MASK_VALUE)

        ckv_c = ckv_ref[0, pl.ds(s0, kc), :]
        d_blk = [jnp.clip(i - (c * sub + j), 0, 2) for j in range(sub)]
        for hp in range(H_A // 2):
            logits = _dot(ckv_c, qlat_ref[:, hp * 2 * bq:(hp + 1) * 2 * bq])
            for hh in range(2):
                h = 2 * hp + hh
                hs = slice(h * bq, (h + 1) * bq)
                m_fold = None
                for j in range(sub):
                    rs = slice(j * bq, (j + 1) * bq)
                    lt = logits[rs, hh * bq:(hh + 1) * bq] + bias_ref[d_blk[j], h] + mask_add[rs, :]
                    lt_ref[slot, rs, hs] = lt
                    lt_fold = sublane_fold(lt, jnp.max)
                    m_fold = lt_fold if m_fold is None else jnp.maximum(m_fold, lt_fold)
                m_old = m_ref[:, hs]
                m_new = jnp.maximum(m_old, jnp.max(m_fold, axis=0, keepdims=True))
                alpha_ref[slot, :, hs] = jnp.exp2(m_old - m_new)
                mnew_ref[slot, :, hs] = m_new
                m_ref[:, hs] = m_new
        return ties_before

    def stage_accumulate(c, slot):
        s0 = pl.multiple_of(c * kc, kc)
        ckvt_c = ckvt_ref[:, pl.ds(s0, kc)]
        for hp in range(H_A // 2):
            cs = slice(hp * 2 * bq, (hp + 1) * 2 * bq)
            p = jnp.exp2(lt_ref[slot, :, cs] - mnew_ref[slot, :, cs])
            acc_ref[:, cs] = acc_ref[:, cs] * alpha_ref[slot, :, cs] + _dot(ckvt_c, p.astype(_BF16))

    def attend(use_rank):
        first = functools.partial(stage_logits, use_rank=use_rank)
        _two_stage_pipeline(n_chunks, first, stage_accumulate, jnp.zeros((1, bq), _F32))

    lax.cond(ranked, lambda: attend(True), lambda: attend(False))

    inv_l = 1.0 / acc_ref[DC_A:DC_A + 1, :]
    o_lat_t = (acc_ref[0:DC_A, :] * inv_l).astype(_BF16)
    o_rows = jnp.concatenate([o_lat_t[:, h * bq:(h + 1) * bq] for h in range(H_A)], axis=0)
    o_ref[0] = _dot_tn(o_rows, wuv_ref[...]).astype(_BF16)


def _dsa(qa, iq, iwt, ik, ckv, ckvt, wukt, wuv, bucket_tbl, rel_bias, tri, *, k_top, kc):
    bsz, seq, _ = qa.shape
    bq = BLOCK_Q
    q_spec = pl.BlockSpec((1, bq, H_A * DH_A), lambda b, i: (b, i, 0))
    return pl.pallas_call(
        functools.partial(_dsa_kernel, k_top=k_top, kc=kc),
        out_shape=jax.ShapeDtypeStruct((bsz, seq, H_A * DH_A), _BF16),
        grid=(bsz, seq // bq),
        in_specs=[q_spec, q_spec,
                  pl.BlockSpec((1, H_IDX, bq), lambda b, i: (b, 0, i)),
                  pl.BlockSpec((1, seq, DH_IDX), lambda b, i: (b, 0, 0)),
                  pl.BlockSpec((1, seq, DC_A), lambda b, i: (b, 0, 0)),
                  pl.BlockSpec((DC_AUG, seq), lambda b, i: (0, b)),
                  _const_spec(wukt.shape), _const_spec(wuv.shape), _const_spec(bucket_tbl.shape),
                  pl.BlockSpec(memory_space=pltpu.SMEM),
                  _const_spec(tri.shape)],
        out_specs=q_spec,
        scratch_shapes=[pltpu.VMEM((seq, bq), _I32),
                        pltpu.VMEM((32, seq // 32, bq), _I32),
                        pltpu.VMEM((3, H_A, bq, bq), _F32),
                        pltpu.VMEM((DC_A, H_A * bq), _BF16),
                        pltpu.VMEM((2, kc, H_A * bq), _F32),
                        pltpu.VMEM((2, 1, H_A * bq), _F32),
                        pltpu.VMEM((2, 1, H_A * bq), _F32),
                        pltpu.VMEM((DC_AUG, H_A * bq), _F32),
                        pltpu.VMEM((1, H_A * bq), _F32)],
        compiler_params=_params(2),
        name="dsa",
    )(qa, iq, iwt, ik, ckv, ckvt, wukt, wuv, bucket_tbl, rel_bias, tri)


def _sb_kernel(q_ref, k_ref, v_ref, tri2_ref, o_ref, acc_ref, run_ref):
    bq = BLOCK_Q
    i = pl.program_id(1)
    q = q_ref[0]
    scale = DH_SB ** -0.5
    run_ref[...] = jnp.zeros(run_ref.shape, _F32)
    acc_ref[...] = jnp.zeros(acc_ref.shape, _F32)
    row = lax.broadcasted_iota(_I32, (bq, bq), 0)
    col = lax.broadcasted_iota(_I32, (bq, bq), 1)

    def cond(state):
        j, run_max = state
        return (j >= 0) & (run_max > _SB_DEAD_LOG)

    def block_terms(jb):
        s0 = pl.multiple_of(jb * bq, bq)
        kb = k_ref[0, pl.ds(s0, bq), :]
        strict = row + (jb - i) * bq < col
        ys, log_nots = [], []
        for h in range(H_SB):
            hs = slice(h * DH_SB, (h + 1) * DH_SB)
            y = _dot_nt(kb[:, hs], q[:, hs]) * (-scale * _LOG2E)
            y = jnp.where(strict, y, -_MASK_VALUE)
            ys.append(y)
            log_nots.append(jnp.minimum(y, 0.0) - jnp.log2(1.0 + jnp.exp2(-jnp.abs(y))))
        log_not = jnp.concatenate(log_nots, axis=1)
        hi = log_not.astype(_BF16)
        lo = (log_not - hi.astype(_F32)).astype(_BF16)
        later = _dot(tri2_ref[...], jnp.concatenate([hi, lo], axis=0))
        fold = jnp.sum(log_not.reshape(bq // V7X_SUBLANES, V7X_SUBLANES, log_not.shape[1]), axis=0)
        return ys, log_nots, later, jnp.sum(fold, axis=0, keepdims=True), s0

    def block_out(ys, log_nots, later, s0):
        vb = v_ref[0, pl.ds(s0, bq), :]
        outs = []
        for h in range(H_SB):
            a = jnp.exp2(log_nots[h] - ys[h] + later[:, h * bq:(h + 1) * bq])
            outs.append(_dot_tn(vb[:, h * DH_SB:(h + 1) * DH_SB], a.astype(_BF16)))
        return jnp.concatenate(outs, axis=0)

    def body(state):
        j, _ = state
        ys, log_nots, later, total, s0 = block_terms(j)
        run = run_ref[...]
        acc_ref[...] += block_out(ys, log_nots, later + run, s0)
        run = run + total
        run_ref[...] = run
        return j - 1, jnp.max(run)

    lax.while_loop(cond, body, (i, jnp.float32(0.0)))
    o_ref[0] = acc_ref[...].T.astype(_BF16)


def _stick_breaking(q, k, v, tri2):
    bsz, seq, width = q.shape
    bq = BLOCK_Q
    return pl.pallas_call(
        _sb_kernel,
        out_shape=jax.ShapeDtypeStruct((bsz, seq, width), _BF16),
        grid=(bsz, seq // bq),
        in_specs=[pl.BlockSpec((1, bq, width), lambda b, i: (b, i, 0)),
                  pl.BlockSpec((1, seq, width), lambda b, i: (b, 0, 0)),
                  pl.BlockSpec((1, seq, width), lambda b, i: (b, 0, 0)),
                  _const_spec(tri2.shape)],
        out_specs=pl.BlockSpec((1, bq, width), lambda b, i: (b, i, 0)),
        scratch_shapes=[pltpu.VMEM((width, bq), _F32), pltpu.VMEM((1, H_SB * bq), _F32)],
        compiler_params=_params(2),
        name="stick_breaking",
    )(q, k, v, tri2)


def _merge_ln_kernel(x_ref, ya_ref, yb_ref, wcg_ref, convw_ref, bgate_ref, wbr_ref, wout_ref,
                     g_ref, b_ref, o_ref, halo_ref, *, alpha, tiles_per_seq):
    i = pl.program_id(0)
    x = x_ref[...]
    xb = x.astype(_BF16)
    tm = x.shape[0]
    c = C_CONV

    @pl.when(i % tiles_per_seq == 0)
    def _():
        halo_ref[...] = jnp.zeros(halo_ref.shape, _F32)

    cb = _dot(xb, wcg_ref[:, 0:c])
    z = _dot(xb, wcg_ref[:, c:2 * c]) * _dot(xb, wcg_ref[:, 2 * c:3 * c])
    halo = halo_ref[...]
    row = lax.broadcasted_iota(_I32, (tm, c), 0)
    last = V7X_SUBLANES - 1
    z1 = jnp.where(row == 0, halo[last:last + 1, :], pltpu.roll(z, 1, 0))
    z2 = jnp.where(row == 0, halo[last - 1:last, :],
                   jnp.where(row == 1, halo[last:last + 1, :], pltpu.roll(z, 2, 0)))
    halo_ref[...] = z[tm - V7X_SUBLANES:, :]
    y_c = cb * (convw_ref[0:1, :] * z2 + convw_ref[1:2, :] * z1 + convw_ref[2:3, :] * z)

    d = x.shape[1]
    merged = jnp.zeros((tm, d), _F32)
    branches = (ya_ref[...], yb_ref[...], y_c.astype(_BF16))
    for n, y in enumerate(branches):
        gate = _dot(xb, wcg_ref[:, 3 * c + n * d:3 * c + (n + 1) * d]) + bgate_ref[n:n + 1, :]
        merged = merged + jax.nn.sigmoid(gate) * _dot(y, wbr_ref[n])
    y = alpha * x + _dot(merged.astype(_BF16), wout_ref[...])
    o_ref[...] = _layer_norm(y, g_ref[...], b_ref[...])


def _merge_ln(x, ya, yb, wcg, convw, bgate, wbr, wout, g, b, *, alpha, tm, seq):
    t, d = x.shape
    return pl.pallas_call(
        functools.partial(_merge_ln_kernel, alpha=alpha, tiles_per_seq=seq // tm),
        out_shape=jax.ShapeDtypeStruct((t, d), _F32),
        grid=(t // tm,),
        in_specs=[pl.BlockSpec((tm, d), lambda i: (i, 0)),
                  pl.BlockSpec((tm, W_BRANCH), lambda i: (i, 0)),
                  pl.BlockSpec((tm, W_BRANCH), lambda i: (i, 0)),
                  _const_spec(wcg.shape), _const_spec(convw.shape), _const_spec(bgate.shape),
                  _const_spec(wbr.shape), _const_spec(wout.shape),
                  _const_spec((1, d)), _const_spec((1, d))],
        out_specs=pl.BlockSpec((tm, d), lambda i: (i, 0)),
        scratch_shapes=[pltpu.VMEM((V7X_SUBLANES, C_CONV), _F32)],
        compiler_params=_params(1),
        name="merge_ln",
    )(x, ya, yb, wcg, convw, bgate, wbr, wout, g, b)


def _mem_kv_kernel(mem_ref, wkv_ref, k_ref, v_ref):
    kv = _dot(mem_ref[0].astype(_BF16), wkv_ref[...])
    w = H_X * DH_X
    k_ref[0] = kv[:, :w].astype(_BF16)
    v_ref[0] = kv[:, w:].astype(_BF16)


def _mem_kv(mem, wkv):
    bsz, m, d = mem.shape
    w = H_X * DH_X
    spec = pl.BlockSpec((1, m, w), lambda b: (b, 0, 0))
    return pl.pallas_call(
        _mem_kv_kernel,
        out_shape=[jax.ShapeDtypeStruct((bsz, m, w), _BF16)] * 2,
        grid=(bsz,),
        in_specs=[pl.BlockSpec((1, m, d), lambda b: (b, 0, 0)), _const_spec(wkv.shape)],
        out_specs=[spec, spec],
        compiler_params=_params(1),
        name="mem_kv",
    )(mem, wkv)


def _xattn_ln_kernel(x_ref, k_ref, v_ref, wq_ref, wo_ref, g_ref, b_ref, o_ref, *, alpha):
    x = x_ref[...]
    q = _dot(x.astype(_BF16), wq_ref[...]).astype(_BF16)
    k = k_ref[0]
    v = v_ref[0]
    outs = []
    for h in range(H_X):
        hs = slice(h * DH_X, (h + 1) * DH_X)
        lt = _dot_nt(k[:, hs], q[:, hs]) * (DH_X ** -0.5)
        p = jnp.exp(lt - jnp.max(lt, axis=0, keepdims=True))
        p = p * (1.0 / jnp.sum(p, axis=0, keepdims=True))
        outs.append(_dot_tn(p.astype(_BF16), v[:, hs]))
    o = jnp.concatenate(outs, axis=1).astype(_BF16)
    y = alpha * x + _dot(o, wo_ref[...])
    o_ref[...] = _layer_norm(y, g_ref[...], b_ref[...])


def _xattn_ln(x, k, v, wq, wo, g, b, *, alpha, tm, seq):
    t, d = x.shape
    m, w = k.shape[1], k.shape[2]
    tiles_per_seq = seq // tm
    kv_spec = pl.BlockSpec((1, m, w), lambda i: (i // tiles_per_seq, 0, 0))
    return pl.pallas_call(
        functools.partial(_xattn_ln_kernel, alpha=alpha),
        out_shape=jax.ShapeDtypeStruct((t, d), _F32),
        grid=(t // tm,),
        in_specs=[pl.BlockSpec((tm, d), lambda i: (i, 0)), kv_spec, kv_spec,
                  _const_spec(wq.shape), _const_spec(wo.shape),
                  _const_spec((1, d)), _const_spec((1, d))],
        out_specs=pl.BlockSpec((tm, d), lambda i: (i, 0)),
        compiler_params=_params(1),
        name="xattn_ln",
    )(x, k, v, wq, wo, g, b)


def _t5_bucket(n):
    max_exact = N_BUCKETS // 2
    n = np.maximum(n, 0)
    nf = np.maximum(n, 1).astype(np.float32)
    ratio = np.log(nf / np.float32(max_exact)) / np.float32(math.log(MAX_DISTANCE / max_exact))
    large = max_exact + (ratio * np.float32(N_BUCKETS - max_exact)).astype(np.int32)
    large = np.minimum(large, N_BUCKETS - 1)
    return np.where(n < max_exact, n, large).astype(np.int32)


def _bucket_tiles():
    bq = BLOCK_Q
    s = np.arange(bq, dtype=np.int32)[:, None]
    t = np.arange(bq, dtype=np.int32)[None, :]
    return np.stack([_t5_bucket(d * bq + t - s) for d in range(3)])


def _strict_upper(n):
    r = np.arange(n)
    return (r[None, :] > r[:, None])


def kernel(x, mem, ln_g, ln_b, ffn_w_in, ffn_w_out, w_mix_in, b_gate, kv_norm_g, w_uk, w_uv, conv_w,
           w_branch, w_mix_out, xa_wq, xa_wkv, xa_wo, rel_bias):
    bsz, seq, d = x.shape
    depth = ln_g.shape[0]
    d_ff = ffn_w_out.shape[2]
    t = bsz * seq
    alpha = (2 * depth) ** 0.25
    k_top = min(TOPK_MAX, seq // 4)
    bq = BLOCK_Q
    kc = min(4 * bq, seq)
    tm = min(512, seq)
    tf = 256

    offs = np.cumsum((H_A * DH_A, DC_A, H_IDX * DH_IDX, DH_IDX, H_IDX, W_BRANCH, W_BRANCH, W_BRANCH,
                      C_CONV, C_CONV, C_CONV, d, d, d))
    o_qa, o_ckv, o_iq, o_ik, o_iw, o_qs, o_ks, o_vs, o_cb = (0,) + tuple(int(o) for o in offs[:8])

    bucket_tbl = jnp.asarray(_bucket_tiles())
    tri = jnp.asarray(_strict_upper(kc).T, _BF16)
    up = _strict_upper(bq)
    tri2 = jnp.asarray(np.concatenate([up, up], axis=1), _BF16)

    x = x.reshape(t, d)
    for l in range(depth):
        g = ln_g[l][:, None, :]
        b = ln_b[l][:, None, :]
        bf = lambda a: a.astype(_BF16)

        x = _ffn_ln(x, bf(ffn_w_in[l, 0]), bf(ffn_w_out[l, 0]), g[0], b[0], alpha=alpha, tm=tm, tf=tf)

        wm = w_mix_in[l]
        pad = jnp.zeros((d, V7X_LANES - DH_IDX - H_IDX), wm.dtype)
        w_proj = jnp.concatenate(
            [wm[:, o_qa:o_ckv], wm[:, o_iq:o_ik], wm[:, o_qs:o_ks], wm[:, o_ks:o_vs], wm[:, o_vs:o_cb],
             wm[:, o_ckv:o_iq], wm[:, o_ik:o_qs], pad], axis=1)
        qa, iq, qs, ks, vs, ckv, ckvt, ikw = _mix_proj(x, bf(w_proj), kv_norm_g[l][None, :], tm=tm)
        ikw = ikw.reshape(bsz, seq, V7X_LANES)
        ik = ikw[:, :, :DH_IDX].astype(_BF16)
        iwt = jnp.swapaxes(ikw[:, :, DH_IDX:DH_IDX + H_IDX], 1, 2)
        r3 = lambda a: a.reshape(bsz, seq, a.shape[-1])
        wuk_bd = jax.scipy.linalg.block_diag(*[w_uk[l, h].T for h in range(H_A)])
        wuv_bd = jax.scipy.linalg.block_diag(*[w_uv[l, h] for h in range(H_A)])
        y_a = _dsa(r3(qa), r3(iq), iwt, ik, r3(ckv), ckvt, bf(wuk_bd), bf(wuv_bd),
                   bucket_tbl, rel_bias, tri, k_top=k_top, kc=kc)
        y_b = _stick_breaking(r3(qs), r3(ks), r3(vs), tri2)
        x = _merge_ln(x, y_a.reshape(t, -1), y_b.reshape(t, -1), bf(wm[:, o_cb:]), conv_w[l], b_gate[l],
                      bf(w_branch[l]), bf(w_mix_out[l]), g[1], b[1], alpha=alpha, tm=tm, seq=seq)

        k_mem, v_mem = _mem_kv(mem, bf(xa_wkv[l]))
        x = _xattn_ln(x, k_mem, v_mem, bf(xa_wq[l]), bf(xa_wo[l]), g[2], b[2], alpha=alpha, tm=tm, seq=seq)

        x = _ffn_ln(x, bf(ffn_w_in[l, 1]), bf(ffn_w_out[l, 1]), g[3], b[3], alpha=alpha, tm=tm, tf=tf)
    return x.reshape(bsz, seq, d)
```

```python
import functools
import math

import jax
import jax.numpy as jnp
import numpy as np
from jax import lax
from jax.experimental import pallas as pl
from jax.experimental.pallas import tpu as pltpu

BLOCK_Q = 128
H_A, DH_A, DC_A = 8, 64, 128
DC_AUG = DC_A + 16
H_IDX, DH_IDX = 8, 64
TOPK_MAX = 256
H_SB, DH_SB = 8, 64
C_CONV, CONV_W = 512, 3
W_BRANCH = 512
H_X, DH_X = 4, 128
N_BUCKETS, MAX_DISTANCE = 32, 128
LN_EPS = 1e-5

V7X_LANES = 128
V7X_SUBLANES = 8
V7X_VMEM_LIMIT_BYTES = 56 * 1024 * 1024

_F32 = jnp.float32
_BF16 = jnp.bfloat16
_I32 = jnp.int32

_INT_MIN = -(2 ** 31)
_NEG_INF_KEY = np.int32(np.uint32(0x807FFFFF).view(np.int32))
_MASK_VALUE = -1e30
_LOG2E = math.log2(math.e)
_SB_DEAD_LOG = -160.0


def _dot(a, b):
    return jnp.dot(a, b, preferred_element_type=_F32)


def _dot_nt(a, b):
    return lax.dot_general(a, b, (((1,), (1,)), ((), ())), preferred_element_type=_F32)


def _dot_tn(a, b):
    return lax.dot_general(a, b, (((0,), (0,)), ((), ())), preferred_element_type=_F32)


def _layer_norm(y, g, b):
    mu = jnp.mean(y, axis=-1, keepdims=True)
    yc = y - mu
    var = jnp.mean(yc * yc, axis=-1, keepdims=True)
    return yc * lax.rsqrt(var + LN_EPS) * g + b


def _params(n_grid, **kw):
    return pltpu.CompilerParams(
        dimension_semantics=("arbitrary",) * n_grid,
        vmem_limit_bytes=V7X_VMEM_LIMIT_BYTES, **kw)


def _const_spec(shape):
    n = len(shape)
    return pl.BlockSpec(shape, lambda *_: (0,) * n, pipeline_mode=pl.Buffered(1))


def _ffn_ln_kernel(x_ref, wi_ref, wo_ref, g_ref, b_ref, o_ref, *, alpha, tf):
    x = x_ref[...]
    xb = x.astype(_BF16)
    d_ff = wo_ref.shape[0]
    acc = jnp.zeros(x.shape, _F32)
    for c in range(d_ff // tf):
        a = _dot(xb, wi_ref[:, c * tf:(c + 1) * tf])
        b = _dot(xb, wi_ref[:, d_ff + c * tf:d_ff + (c + 1) * tf])
        h = (a * jax.nn.sigmoid(a) * b).astype(_BF16)
        acc = acc + _dot(h, wo_ref[c * tf:(c + 1) * tf, :])
    y = alpha * x + 0.5 * acc
    o_ref[...] = _layer_norm(y, g_ref[...], b_ref[...])


def _ffn_ln(x, wi, wo, g, b, *, alpha, tm, tf):
    t, d = x.shape
    f = wo.shape[0]
    return pl.pallas_call(
        functools.partial(_ffn_ln_kernel, alpha=alpha, tf=tf),
        out_shape=jax.ShapeDtypeStruct((t, d), _F32),
        grid=(t // tm,),
        in_specs=[pl.BlockSpec((tm, d), lambda i: (i, 0)),
                  _const_spec((d, 2 * f)), _const_spec((f, d)),
                  _const_spec((1, d)), _const_spec((1, d))],
        out_specs=pl.BlockSpec((tm, d), lambda i: (i, 0)),
        compiler_params=_params(1),
        name="ffn_ln",
    )(x, wi, wo, g, b)


def _mix_proj_kernel(x_ref, w_ref, kvg_ref, qa_ref, iq_ref, qs_ref, ks_ref, vs_ref, ckv_ref, ckvt_ref,
                     ikw_ref):
    xb = x_ref[...].astype(_BF16)
    wide = W_BRANCH
    for j, ref in enumerate((qa_ref, iq_ref, qs_ref, ks_ref, vs_ref)):
        ref[...] = _dot(xb, w_ref[:, j * wide:(j + 1) * wide]).astype(_BF16)
    off = 5 * wide
    ckv = _dot(xb, w_ref[:, off:off + DC_A])
    ckv = ckv * lax.rsqrt(jnp.mean(ckv * ckv, axis=-1, keepdims=True) + LN_EPS) * kvg_ref[...]
    ckv_ref[...] = ckv.astype(_BF16)
    ones = jnp.ones((DC_AUG - DC_A, ckv.shape[0]), _F32)
    ckvt_ref[...] = jnp.concatenate([ckv.T, ones], axis=0).astype(_BF16)
    ikw_ref[...] = _dot(xb, w_ref[:, off + DC_A:off + DC_A + V7X_LANES])


def _mix_proj(x, w, kvg, *, tm):
    t, d = x.shape
    n = w.shape[1]
    wide_spec = pl.BlockSpec((tm, W_BRANCH), lambda i: (i, 0))
    lane_spec = pl.BlockSpec((tm, V7X_LANES), lambda i: (i, 0))
    return pl.pallas_call(
        _mix_proj_kernel,
        out_shape=[jax.ShapeDtypeStruct((t, W_BRANCH), _BF16)] * 5
        + [jax.ShapeDtypeStruct((t, DC_A), _BF16), jax.ShapeDtypeStruct((DC_AUG, t), _BF16),
           jax.ShapeDtypeStruct((t, V7X_LANES), _F32)],
        grid=(t // tm,),
        in_specs=[pl.BlockSpec((tm, d), lambda i: (i, 0)), _const_spec((d, n)), _const_spec((1, DC_A))],
        out_specs=[wide_spec] * 5 + [lane_spec, pl.BlockSpec((DC_AUG, tm), lambda i: (0, i)), lane_spec],
        compiler_params=_params(1),
        name="mix_proj",
    )(x, w, kvg)


def _bit_transpose32(words):
    a = list(words)
    j, m = 16, 0x0000FFFF
    while j:
        mask = np.int32(np.uint32(m).view(np.int32))
        for k in range(32):
            if k & j == 0:
                t = ((a[k] >> j) ^ a[k + j]) & mask
                a[k] = a[k] ^ (t << j)
                a[k + j] = a[k + j] ^ t
        j >>= 1
        m ^= (m << j) & 0xFFFFFFFF
    return a


def _two_stage_pipeline(n, first, second, carry):
    carry = first(0, 0, carry)

    def pair(k, carry):
        c = 2 * k
        carry = first(c + 1, 1, carry)
        second(c, 0)
        carry = first(c + 2, 0, carry)
        second(c + 1, 1)
        return carry

    n_pairs = (n - 1) // 2
    carry = lax.fori_loop(0, n_pairs, pair, carry)
    c_tail = 2 * n_pairs

    def even_tail():
        first(c_tail + 1, 1, carry)
        second(c_tail, 0)
        second(c_tail + 1, 1)

    def odd_tail():
        second(c_tail, 0)

    lax.cond(n - c_tail == 2, even_tail, odd_tail)


def _dsa_kernel(qa_ref, iq_ref, iwt_ref, ik_ref, ckv_ref, ckvt_ref, wukt_ref, wuv_ref, bucket_ref,
                relb_ref, tri_ref, o_ref, keys_ref, planes_ref, bias_ref, qlat_ref, lt_ref, alpha_ref,
                mnew_ref, acc_ref, m_ref, *, k_top, kc):
    bq = BLOCK_Q
    b_idx = pl.program_id(0)
    i = pl.program_id(1)
    q0 = i * bq
    n_chunks = (q0 + bq + kc - 1) // kc
    sub = kc // bq
    groups = kc // 256
    words_c = kc // 32
    n_words = planes_ref.shape[1]

    @pl.when((b_idx == 0) & (i == 0))
    def _():
        planes_ref[...] = jnp.zeros(planes_ref.shape, _I32)
        bias_ref[...] = jnp.zeros(bias_ref.shape, _F32)

        def fill(bkt, carry):
            hit = bucket_ref[...] == bkt
            for h in range(H_A):
                bias_ref[:, h] = jnp.where(hit, relb_ref[bkt, h] * _LOG2E, bias_ref[:, h])
            return carry

        lax.fori_loop(0, N_BUCKETS, fill, 0)

    qa = qa_ref[0]
    iq = iq_ref[0]
    iq_rows = jnp.concatenate([iq[:, h * DH_IDX:(h + 1) * DH_IDX] for h in range(H_IDX)], axis=0)
    w_idx = iwt_ref[0] * ((H_IDX ** -0.5) * (DH_IDX ** -0.5))
    t_idx = q0 + lax.broadcasted_iota(_I32, (1, bq), 1)

    def stage_dots(c, slot, carry):
        s0 = pl.multiple_of(c * kc, kc)
        lt_ref[slot] = _dot_nt(ik_ref[0, pl.ds(s0, kc), :], iq_rows)
        return carry

    def stage_keys(c, slot):
        grp = kc // groups
        for g in range(groups):
            s0 = pl.multiple_of(c * kc + g * grp, grp)
            rows = slice(g * grp, (g + 1) * grp)
            sc = jnp.zeros((grp, bq), _F32)
            for h in range(H_IDX):
                sc = sc + jnp.maximum(lt_ref[slot, rows, h * bq:(h + 1) * bq], 0.0) * w_idx[h:h + 1, :]
            s_idx = s0 + lax.broadcasted_iota(_I32, (grp, bq), 0)
            sc = jnp.where(s_idx <= t_idx, sc, -jnp.inf)
            bits = pltpu.bitcast(sc, _I32)
            keys = bits ^ ((bits >> 31) & 0x7FFFFFFF)
            keys_ref[pl.ds(s0, grp), :] = keys
            u = (keys ^ _INT_MIN).reshape(32, V7X_SUBLANES, bq)
            planes = _bit_transpose32([u[j] for j in range(32)])
            w0 = pl.multiple_of((c * groups + g) * V7X_SUBLANES, V7X_SUBLANES)
            for b in range(32):
                planes_ref[b, pl.ds(w0, V7X_SUBLANES), :] = planes[b]

    _two_stage_pipeline(n_chunks, stage_dots, stage_keys, 0)

    word_row = lax.broadcasted_iota(_I32, (n_words, bq), 0)
    alive0 = jnp.where(word_row < n_chunks * words_c, -1, 0)

    def bit_step(it, state):
        alive, need, thr_u = state
        b = 31 - it
        ones = alive & planes_ref[b]
        cnt = lax.population_count(ones)
        cnt = jnp.sum(cnt.reshape(n_words // V7X_SUBLANES, V7X_SUBLANES, bq), axis=0)
        cnt = jnp.sum(cnt, axis=0, keepdims=True)
        take = cnt >= need
        alive = jnp.where(take, ones, alive ^ ones)
        need = jnp.where(take, need, need - cnt)
        thr_u = thr_u | jnp.where(take, lax.shift_left(jnp.int32(1), b), 0)
        return alive, need, thr_u

    def sublane_fold(a, op):
        return op(a.reshape(a.shape[0] // V7X_SUBLANES, V7X_SUBLANES, a.shape[1]), axis=0)

    alive, need, thr_u = lax.fori_loop(
        0, 32, bit_step, (alive0, jnp.full((1, bq), k_top, _I32), jnp.zeros((1, bq), _I32)))
    thr = thr_u ^ _INT_MIN
    n_ties = jnp.where(thr > _NEG_INF_KEY, need, 0)
    thr_gt = jnp.maximum(thr, _NEG_INF_KEY)
    n_equal = jnp.sum(sublane_fold(lax.population_count(alive), jnp.sum), axis=0, keepdims=True)
    ranked = jnp.max(jnp.where(thr > _NEG_INF_KEY, n_equal - need, 0)) > 0

    q_rows = _dot_nt(wukt_ref[...], qa) * ((DH_A ** -0.5) * _LOG2E)
    for h in range(H_A):
        qlat_ref[:, h * bq:(h + 1) * bq] = q_rows[h * DC_A:(h + 1) * DC_A, :].astype(_BF16)

    m_ref[...] = jnp.full(m_ref.shape, -jnp.inf, _F32)
    acc_ref[...] = jnp.zeros(acc_ref.shape, _F32)

    def stage_logits(c, slot, ties_before, use_rank):
        s0 = pl.multiple_of(c * kc, kc)
        keys = keys_ref[pl.ds(s0, kc), :]
        if use_rank:
            eq = keys == thr
            eq_f = jnp.where(eq, 1.0, 0.0)
            rank = ties_before + _dot(tri_ref[...], eq_f.astype(_BF16))
            sel = (keys > thr_gt) | (eq & (rank < n_ties.astype(_F32)))
            ties_before = ties_before + jnp.sum(sublane_fold(eq_f, jnp.sum), axis=0, keepdims=True)
        else:
            sel = keys >= jnp.maximum(thr, _NEG_INF_KEY + 1)
        mask_add = jnp.where(sel, 0.0, _MASK_VALUE)

        ckv_c = ckv_ref[0, pl.ds(s0, kc), :]
        d_blk = [jnp.clip(i - (c * sub + j), 0, 2) for j in range(sub)]
        for hp in range(H_A // 2):
            logits = _dot(ckv_c, qlat_ref[:, hp * 2 * bq:(hp + 1) * 2 * bq])
            for hh in range(2):
                h = 2 * hp + hh
                hs = slice(h * bq, (h + 1) * bq)
                m_fold = None
                for j in range(sub):
                    rs = slice(j * bq, (j + 1) * bq)
                    lt = logits[rs, hh * bq:(hh + 1) * bq] + bias_ref[d_blk[j], h] + mask_add[rs, :]
                    lt_ref[slot, rs, hs] = lt
                    lt_fold = sublane_fold(lt, jnp.max)
                    m_fold = lt_fold if m_fold is None else jnp.maximum(m_fold, lt_fold)
                m_old = m_ref[:, hs]
                m_new = jnp.maximum(m_old, jnp.max(m_fold, axis=0, keepdims=True))
                alpha_ref[slot, :, hs] = jnp.exp2(m_old - m_new)
                mnew_ref[slot, :, hs] = m_new
                m_ref[:, hs] = m_new
        return ties_before

    def stage_accumulate(c, slot):
        s0 = pl.multiple_of(c * kc, kc)
        ckvt_c = ckvt_ref[:, pl.ds(s0, kc)]
        for hp in range(H_A // 2):
            cs = slice(hp * 2 * bq, (hp + 1) * 2 * bq)
            p = jnp.exp2(lt_ref[slot, :, cs] - mnew_ref[slot, :, cs])
            acc_ref[:, cs] = acc_ref[:, cs] * alpha_ref[slot, :, cs] + _dot(ckvt_c, p.astype(_BF16))

    def attend(use_rank):
        first = functools.partial(stage_logits, use_rank=use_rank)
        _two_stage_pipeline(n_chunks, first, stage_accumulate, jnp.zeros((1, bq), _F32))

    lax.cond(ranked, lambda: attend(True), lambda: attend(False))

    inv_l = 1.0 / acc_ref[DC_A:DC_A + 1, :]
    o_lat_t = (acc_ref[0:DC_A, :] * inv_l).astype(_BF16)
    o_rows = jnp.concatenate([o_lat_t[:, h * bq:(h + 1) * bq] for h in range(H_A)], axis=0)
    o_ref[0] = _dot_tn(o_rows, wuv_ref[...]).astype(_BF16)


def _dsa(qa, iq, iwt, ik, ckv, ckvt, wukt, wuv, bucket_tbl, rel_bias, tri, *, k_top, kc):
    bsz, seq, _ = qa.shape
    bq = BLOCK_Q
    q_spec = pl.BlockSpec((1, bq, H_A * DH_A), lambda b, i: (b, i, 0))
    return pl.pallas_call(
        functools.partial(_dsa_kernel, k_top=k_top, kc=kc),
        out_shape=jax.ShapeDtypeStruct((bsz, seq, H_A * DH_A), _BF16),
        grid=(bsz, seq // bq),
        in_specs=[q_spec, q_spec,
                  pl.BlockSpec((1, H_IDX, bq), lambda b, i: (b, 0, i)),
                  pl.BlockSpec((1, seq, DH_IDX), lambda b, i: (b, 0, 0)),
                  pl.BlockSpec((1, seq, DC_A), lambda b, i: (b, 0, 0)),
                  pl.BlockSpec((DC_AUG, seq), lambda b, i: (0, b)),
                  _const_spec(wukt.shape), _const_spec(wuv.shape), _const_spec(bucket_tbl.shape),
                  pl.BlockSpec(memory_space=pltpu.SMEM),
                  _const_spec(tri.shape)],
        out_specs=q_spec,
        scratch_shapes=[pltpu.VMEM((seq, bq), _I32),
                        pltpu.VMEM((32, seq // 32, bq), _I32),
                        pltpu.VMEM((3, H_A, bq, bq), _F32),
                        pltpu.VMEM((DC_A, H_A * bq), _BF16),
                        pltpu.VMEM((2, kc, H_A * bq), _F32),
                        pltpu.VMEM((2, 1, H_A * bq), _F32),
                        pltpu.VMEM((2, 1, H_A * bq), _F32),
                        pltpu.VMEM((DC_AUG, H_A * bq), _F32),
                        pltpu.VMEM((1, H_A * bq), _F32)],
        compiler_params=_params(2),
        name="dsa",
    )(qa, iq, iwt, ik, ckv, ckvt, wukt, wuv, bucket_tbl, rel_bias, tri)


def _sb_kernel(q_ref, k_ref, v_ref, tri2_ref, o_ref, acc_ref, run_ref):
    bq = BLOCK_Q
    i = pl.program_id(1)
    q = q_ref[0]
    scale = DH_SB ** -0.5
    run_ref[...] = jnp.zeros(run_ref.shape, _F32)
    acc_ref[...] = jnp.zeros(acc_ref.shape, _F32)
    row = lax.broadcasted_iota(_I32, (bq, bq), 0)
    col = lax.broadcasted_iota(_I32, (bq, bq), 1)

    def cond(state):
        j, run_max = state
        return (j >= 0) & (run_max > _SB_DEAD_LOG)

    def block_terms(jb):
        s0 = pl.multiple_of(jnp.maximum(jb, 0) * bq, bq)
        kb = k_ref[0, pl.ds(s0, bq), :]
        strict = row + ((jb - i) * bq + jnp.where(jb >= 0, 0, 1 << 30)) < col
        ys, log_nots = [], []
        for h in range(H_SB):
            hs = slice(h * DH_SB, (h + 1) * DH_SB)
            y = _dot_nt(kb[:, hs], q[:, hs]) * (-scale * _LOG2E)
            y = jnp.where(strict, y, -_MASK_VALUE)
            ys.append(y)
            log_nots.append(jnp.minimum(y, 0.0) - jnp.log2(1.0 + jnp.exp2(-jnp.abs(y))))
        log_not = jnp.concatenate(log_nots, axis=1)
        hi = log_not.astype(_BF16)
        lo = (log_not - hi.astype(_F32)).astype(_BF16)
        later = _dot(tri2_ref[...], jnp.concatenate([hi, lo], axis=0))
        fold = jnp.sum(log_not.reshape(bq // V7X_SUBLANES, V7X_SUBLANES, log_not.shape[1]), axis=0)
        return ys, log_nots, later, jnp.sum(fold, axis=0, keepdims=True), s0

    def block_out(ys, log_nots, later, s0):
        vb = v_ref[0, pl.ds(s0, bq), :]
        outs = []
        for h in range(H_SB):
            a = jnp.exp2(log_nots[h] - ys[h] + later[:, h * bq:(h + 1) * bq])
            outs.append(_dot_tn(vb[:, h * DH_SB:(h + 1) * DH_SB], a.astype(_BF16)))
        return jnp.concatenate(outs, axis=0)

    def body(state):
        j, _ = state
        ys0, ln0, later0, tot0, s00 = block_terms(j)
        ys1, ln1, later1, tot1, s01 = block_terms(j - 1)
        run = run_ref[...]
        run_mid = run + tot0
        acc_ref[...] += (block_out(ys0, ln0, later0 + run, s00)
                         + block_out(ys1, ln1, later1 + run_mid, s01))
        run = run_mid + tot1
        run_ref[...] = run
        return j - 2, jnp.max(run)

    lax.while_loop(cond, body, (i, jnp.float32(0.0)))
    o_ref[0] = acc_ref[...].T.astype(_BF16)


def _stick_breaking(q, k, v, tri2):
    bsz, seq, width = q.shape
    bq = BLOCK_Q
    return pl.pallas_call(
        _sb_kernel,
        out_shape=jax.ShapeDtypeStruct((bsz, seq, width), _BF16),
        grid=(bsz, seq // bq),
        in_specs=[pl.BlockSpec((1, bq, width), lambda b, i: (b, i, 0)),
                  pl.BlockSpec((1, seq, width), lambda b, i: (b, 0, 0)),
                  pl.BlockSpec((1, seq, width), lambda b, i: (b, 0, 0)),
                  _const_spec(tri2.shape)],
        out_specs=pl.BlockSpec((1, bq, width), lambda b, i: (b, i, 0)),
        scratch_shapes=[pltpu.VMEM((width, bq), _F32), pltpu.VMEM((1, H_SB * bq), _F32)],
        compiler_params=_params(2),
        name="stick_breaking",
    )(q, k, v, tri2)


def _merge_ln_kernel(x_ref, ya_ref, yb_ref, wcg_ref, convw_ref, bgate_ref, wbr_ref, wout_ref,
                     g_ref, b_ref, o_ref, halo_ref, *, alpha, tiles_per_seq):
    i = pl.program_id(0)
    x = x_ref[...]
    xb = x.astype(_BF16)
    tm = x.shape[0]
    c = C_CONV

    @pl.when(i % tiles_per_seq == 0)
    def _():
        halo_ref[...] = jnp.zeros(halo_ref.shape, _F32)

    cb = _dot(xb, wcg_ref[:, 0:c])
    z = _dot(xb, wcg_ref[:, c:2 * c]) * _dot(xb, wcg_ref[:, 2 * c:3 * c])
    halo = halo_ref[...]
    row = lax.broadcasted_iota(_I32, (tm, c), 0)
    last = V7X_SUBLANES - 1
    z1 = jnp.where(row == 0, halo[last:last + 1, :], pltpu.roll(z, 1, 0))
    z2 = jnp.where(row == 0, halo[last - 1:last, :],
                   jnp.where(row == 1, halo[last:last + 1, :], pltpu.roll(z, 2, 0)))
    halo_ref[...] = z[tm - V7X_SUBLANES:, :]
    y_c = cb * (convw_ref[0:1, :] * z2 + convw_ref[1:2, :] * z1 + convw_ref[2:3, :] * z)

    d = x.shape[1]
    merged = jnp.zeros((tm, d), _F32)
    branches = (ya_ref[...], yb_ref[...], y_c.astype(_BF16))
    for n, y in enumerate(branches):
        gate = _dot(xb, wcg_ref[:, 3 * c + n * d:3 * c + (n + 1) * d]) + bgate_ref[n:n + 1, :]
        merged = merged + jax.nn.sigmoid(gate) * _dot(y, wbr_ref[n])
    y = alpha * x + _dot(merged.astype(_BF16), wout_ref[...])
    o_ref[...] = _layer_norm(y, g_ref[...], b_ref[...])


def _merge_ln(x, ya, yb, wcg, convw, bgate, wbr, wout, g, b, *, alpha, tm, seq):
    t, d = x.shape
    return pl.pallas_call(
        functools.partial(_merge_ln_kernel, alpha=alpha, tiles_per_seq=seq // tm),
        out_shape=jax.ShapeDtypeStruct((t, d), _F32),
        grid=(t // tm,),
        in_specs=[pl.BlockSpec((tm, d), lambda i: (i, 0)),
                  pl.BlockSpec((tm, W_BRANCH), lambda i: (i, 0)),
                  pl.BlockSpec((tm, W_BRANCH), lambda i: (i, 0)),
                  _const_spec(wcg.shape), _const_spec(convw.shape), _const_spec(bgate.shape),
                  _const_spec(wbr.shape), _const_spec(wout.shape),
                  _const_spec((1, d)), _const_spec((1, d))],
        out_specs=pl.BlockSpec((tm, d), lambda i: (i, 0)),
        scratch_shapes=[pltpu.VMEM((V7X_SUBLANES, C_CONV), _F32)],
        compiler_params=_params(1),
        name="merge_ln",
    )(x, ya, yb, wcg, convw, bgate, wbr, wout, g, b)


def _mem_kv_kernel(mem_ref, wkv_ref, k_ref, v_ref):
    kv = _dot(mem_ref[0].astype(_BF16), wkv_ref[...])
    w = H_X * DH_X
    k_ref[0] = kv[:, :w].astype(_BF16)
    v_ref[0] = kv[:, w:].astype(_BF16)


def _mem_kv(mem, wkv):
    bsz, m, d = mem.shape
    w = H_X * DH_X
    spec = pl.BlockSpec((1, m, w), lambda b: (b, 0, 0))
    return pl.pallas_call(
        _mem_kv_kernel,
        out_shape=[jax.ShapeDtypeStruct((bsz, m, w), _BF16)] * 2,
        grid=(bsz,),
        in_specs=[pl.BlockSpec((1, m, d), lambda b: (b, 0, 0)), _const_spec(wkv.shape)],
        out_specs=[spec, spec],
        compiler_params=_params(1),
        name="mem_kv",
    )(mem, wkv)


def _xattn_ln_kernel(x_ref, k_ref, v_ref, wq_ref, wo_ref, g_ref, b_ref, o_ref, *, alpha):
    x = x_ref[...]
    q = _dot(x.astype(_BF16), wq_ref[...]).astype(_BF16)
    k = k_ref[0]
    v = v_ref[0]
    outs = []
    for h in range(H_X):
        hs = slice(h * DH_X, (h + 1) * DH_X)
        lt = _dot_nt(k[:, hs], q[:, hs]) * (DH_X ** -0.5)
        p = jnp.exp(lt - jnp.max(lt, axis=0, keepdims=True))
        p = p * (1.0 / jnp.sum(p, axis=0, keepdims=True))
        outs.append(_dot_tn(p.astype(_BF16), v[:, hs]))
    o = jnp.concatenate(outs, axis=1).astype(_BF16)
    y = alpha * x + _dot(o, wo_ref[...])
    o_ref[...] = _layer_norm(y, g_ref[...], b_ref[...])


def _xattn_ln(x, k, v, wq, wo, g, b, *, alpha, tm, seq):
    t, d = x.shape
    m, w = k.shape[1], k.shape[2]
    tiles_per_seq = seq // tm
    kv_spec = pl.BlockSpec((1, m, w), lambda i: (i // tiles_per_seq, 0, 0))
    return pl.pallas_call(
        functools.partial(_xattn_ln_kernel, alpha=alpha),
        out_shape=jax.ShapeDtypeStruct((t, d), _F32),
        grid=(t // tm,),
        in_specs=[pl.BlockSpec((tm, d), lambda i: (i, 0)), kv_spec, kv_spec,
                  _const_spec(wq.shape), _const_spec(wo.shape),
                  _const_spec((1, d)), _const_spec((1, d))],
        out_specs=pl.BlockSpec((tm, d), lambda i: (i, 0)),
        compiler_params=_params(1),
        name="xattn_ln",
    )(x, k, v, wq, wo, g, b)


def _t5_bucket(n):
    max_exact = N_BUCKETS // 2
    n = np.maximum(n, 0)
    nf = np.maximum(n, 1).astype(np.float32)
    ratio = np.log(nf / np.float32(max_exact)) / np.float32(math.log(MAX_DISTANCE / max_exact))
    large = max_exact + (ratio * np.float32(N_BUCKETS - max_exact)).astype(np.int32)
    large = np.minimum(large, N_BUCKETS - 1)
    return np.where(n < max_exact, n, large).astype(np.int32)


def _bucket_tiles():
    bq = BLOCK_Q
    s = np.arange(bq, dtype=np.int32)[:, None]
    t = np.arange(bq, dtype=np.int32)[None, :]
    return np.stack([_t5_bucket(d * bq + t - s) for d in range(3)])


def _strict_upper(n):
    r = np.arange(n)
    return (r[None, :] > r[:, None])


def kernel(x, mem, ln_g, ln_b, ffn_w_in, ffn_w_out, w_mix_in, b_gate, kv_norm_g, w_uk, w_uv, conv_w,
           w_branch, w_mix_out, xa_wq, xa_wkv, xa_wo, rel_bias):
    bsz, seq, d = x.shape
    depth = ln_g.shape[0]
    d_ff = ffn_w_out.shape[2]
    t = bsz * seq
    alpha = (2 * depth) ** 0.25
    k_top = min(TOPK_MAX, seq // 4)
    bq = BLOCK_Q
    kc = min(4 * bq, seq)
    tm = min(512, seq)
    tf = 256

    offs = np.cumsum((H_A * DH_A, DC_A, H_IDX * DH_IDX, DH_IDX, H_IDX, W_BRANCH, W_BRANCH, W_BRANCH,
                      C_CONV, C_CONV, C_CONV, d, d, d))
    o_qa, o_ckv, o_iq, o_ik, o_iw, o_qs, o_ks, o_vs, o_cb = (0,) + tuple(int(o) for o in offs[:8])

    bucket_tbl = jnp.asarray(_bucket_tiles())
    tri = jnp.asarray(_strict_upper(kc).T, _BF16)
    up = _strict_upper(bq)
    tri2 = jnp.asarray(np.concatenate([up, up], axis=1), _BF16)

    x = x.reshape(t, d)
    for l in range(depth):
        g = ln_g[l][:, None, :]
        b = ln_b[l][:, None, :]
        bf = lambda a: a.astype(_BF16)

        x = _ffn_ln(x, bf(ffn_w_in[l, 0]), bf(ffn_w_out[l, 0]), g[0], b[0], alpha=alpha, tm=tm, tf=tf)

        wm = w_mix_in[l]
        pad = jnp.zeros((d, V7X_LANES - DH_IDX - H_IDX), wm.dtype)
        w_proj = jnp.concatenate(
            [wm[:, o_qa:o_ckv], wm[:, o_iq:o_ik], wm[:, o_qs:o_ks], wm[:, o_ks:o_vs], wm[:, o_vs:o_cb],
             wm[:, o_ckv:o_iq], wm[:, o_ik:o_qs], pad], axis=1)
        qa, iq, qs, ks, vs, ckv, ckvt, ikw = _mix_proj(x, bf(w_proj), kv_norm_g[l][None, :], tm=tm)
        ikw = ikw.reshape(bsz, seq, V7X_LANES)
        ik = ikw[:, :, :DH_IDX].astype(_BF16)
        iwt = jnp.swapaxes(ikw[:, :, DH_IDX:DH_IDX + H_IDX], 1, 2)
        r3 = lambda a: a.reshape(bsz, seq, a.shape[-1])
        wuk_bd = jax.scipy.linalg.block_diag(*[w_uk[l, h].T for h in range(H_A)])
        wuv_bd = jax.scipy.linalg.block_diag(*[w_uv[l, h] for h in range(H_A)])
        y_a = _dsa(r3(qa), r3(iq), iwt, ik, r3(ckv), ckvt, bf(wuk_bd), bf(wuv_bd),
                   bucket_tbl, rel_bias, tri, k_top=k_top, kc=kc)
        y_b = _stick_breaking(r3(qs), r3(ks), r3(vs), tri2)
        x = _merge_ln(x, y_a.reshape(t, -1), y_b.reshape(t, -1), bf(wm[:, o_cb:]), conv_w[l], b_gate[l],
                      bf(w_branch[l]), bf(w_mix_out[l]), g[1], b[1], alpha=alpha, tm=tm, seq=seq)

        k_mem, v_mem = _mem_kv(mem, bf(xa_wkv[l]))
        x = _xattn_ln(x, k_mem, v_mem, bf(xa_wq[l]), bf(xa_wo[l]), g[2], b[2], alpha=alpha, tm=tm, seq=seq)

        x = _ffn_ln(x, bf(ffn_w_in[l, 1]), bf(ffn_w_out[l, 1]), g[3], b[3], alpha=alpha, tm=tm, tf=tf)
    return x.reshape(bsz, seq, d)
```
